```python
import jax, jax.numpy as jnp
from jax import lax
import numpy as np

D_MODEL = 1024
BATCH = 8
SEQ = 2048
DEPTH = 4
DEC_BATCH = 128
DEC_SEQ = 1
PAST_LEN = 8192
PAGE_SIZE = 128

CHUNK = 128
G_HEADS = 4
G_HEAD_DIM = 128
G_WIDTH = G_HEADS * G_HEAD_DIM
M_HEADS = 8
QK_NOPE = 64
QK_ROPE = 32
QK_HEAD = QK_NOPE + QK_ROPE
V_HEAD = 64
M_WIDTH = M_HEADS * V_HEAD
Q_RANK = 384
KV_RANK = 256
ROPE_THETA = 10000.0
Q_BLOCK = 128
ATTN_SCALE = QK_HEAD ** -0.5
MIX_WIDTH = G_WIDTH + M_WIDTH
IN_WIDTH = 2 * G_WIDTH + Q_RANK + KV_RANK + QK_ROPE
SPLITS = (G_WIDTH, 2 * G_WIDTH, 2 * G_WIDTH + Q_RANK, 2 * G_WIDTH + Q_RANK + KV_RANK)
N_EXPERTS = 16
N_GROUPS = 4
EXPERTS_PER_GROUP = N_EXPERTS // N_GROUPS
GROUP_SCORE_TOPK = 2
TOP_K = 2
D_EXPERT = 512
N_MOD = 6
EPS = 1e-6

kernel_name = 'hybrid_gmlp_mla_moe_adaln_step'


def rms_norm(x, g):
    x32 = x.astype(jnp.float32)
    y = x32 * lax.rsqrt(jnp.mean(x32 * x32, axis=-1, keepdims=True) + EPS)
    return (y * g.astype(jnp.float32)).astype(x.dtype)


def rope(x, pos):
    half = x.shape[-1] // 2
    freqs = ROPE_THETA ** (-jnp.arange(half, dtype=jnp.float32) / half)
    ang = pos.astype(jnp.float32)[:, None] * freqs[None, :]
    cos = jnp.cos(ang)[:, None, :]
    sin = jnp.sin(ang)[:, None, :]
    x32 = x.astype(jnp.float32)
    x1, x2 = x32[..., :half], x32[..., half:]
    return jnp.concatenate([x1 * cos - x2 * sin, x1 * sin + x2 * cos], axis=-1).astype(x.dtype)


def ada_modulation(c, w_ada, b_ada):
    m = jax.nn.silu(c) @ w_ada + b_ada
    return jnp.split(m[:, None, :], N_MOD, axis=-1)


def mixer_inputs(h, pos, w_in, g_v, g_cq, w_uq, g_ckv, g_qk_q):
    B, T, _ = h.shape
    z = h @ w_in
    u, v, cq, ckv, kr = jnp.split(z, SPLITS, axis=-1)
    u = jax.nn.gelu(u).reshape(B, T, G_HEADS, G_HEAD_DIM)
    v = rms_norm(jax.nn.gelu(v).reshape(B, T, G_HEADS, G_HEAD_DIM), g_v)
    q = (rms_norm(cq, g_cq) @ w_uq).reshape(B, T, M_HEADS, QK_HEAD)
    q = jnp.concatenate([q[..., :QK_NOPE], rope(q[..., QK_NOPE:], pos)], axis=-1)
    q = rms_norm(q, g_qk_q)
    ckv = rms_norm(ckv, g_ckv)
    kr = rope(kr[:, :, None, :], pos)[:, :, 0, :]
    return u, v, q, ckv, kr


def mla_expand(ckv, kr, w_ukv, g_qk_k):
    lead = ckv.shape[:-1]
    kv = (ckv @ w_ukv).reshape(*lead, M_HEADS, QK_NOPE + V_HEAD)
    k_rope = jnp.broadcast_to(kr[..., None, :], (*lead, M_HEADS, QK_ROPE))
    k = rms_norm(jnp.concatenate([kv[..., :QK_NOPE], k_rope], axis=-1), g_qk_k)
    return k, kv[..., QK_NOPE:]


def prompt_attention(q, k, v):
    B, T, H, Dq = q.shape
    nb = T // Q_BLOCK
    qb = q.reshape(B, nb, Q_BLOCK, H, Dq).transpose(1, 0, 2, 3, 4)
    kpos = jnp.arange(T)

    def one_block(args):
        q_blk, i = args
        qpos = i * Q_BLOCK + jnp.arange(Q_BLOCK)
        s = jnp.einsum('bqhd,bkhd->bhqk', q_blk, k).astype(jnp.float32) * ATTN_SCALE
        s = jnp.where(kpos[None, :] <= qpos[:, None], s, -jnp.inf)
        p = jax.nn.softmax(s, axis=-1).astype(v.dtype)
        return jnp.einsum('bhqk,bkhd->bqhd', p, v)

    o = lax.map(one_block, (qb, jnp.arange(nb)))
    return o.transpose(1, 0, 2, 3, 4).reshape(B, T, H, V_HEAD)


def sample_attention(q, ckv_new, kr_new, cache_kv, cache_kr, page_table, layer, w_ukv, g_qk_k):
    t_new = q.shape[1]
    past = page_table.shape[1] * PAGE_SIZE
    kpos = jnp.arange(past + t_new)
    qpos = past + jnp.arange(t_new)
    mask = kpos[None, :] <= qpos[:, None]

    def one_seq(args):
        q_s, c_new, r_new, pages = args
        c = jnp.concatenate([cache_kv[layer, pages].reshape(past, KV_RANK), c_new], axis=0)
        r = jnp.concatenate([cache_kr[layer, pages].reshape(past, QK_ROPE), r_new], axis=0)
        k, v = mla_expand(c, r, w_ukv, g_qk_k)
        s = jnp.einsum('qhd,khd->hqk', q_s, k).astype(jnp.float32) * ATTN_SCALE
        p = jax.nn.softmax(jnp.where(mask[None], s, -jnp.inf), axis=-1).astype(v.dtype)
        return jnp.einsum('hqk,khd->qhd', p, v)

    return lax.map(one_seq, (q, ckv_new, kr_new, page_table))


def chunk_spatial_gate(u, v, w_s, b_s):
    B, T = u.shape[:2]
    n_chunk = -(-T // CHUNK)
    pad = n_chunk * CHUNK - T
    vp = jnp.pad(v, ((0, 0), (0, pad), (0, 0), (0, 0))).reshape(B, n_chunk, CHUNK, G_HEADS, G_HEAD_DIM)
    causal = jnp.tril(jnp.ones((CHUNK, CHUNK), dtype=bool))
    w = jnp.where(causal[None], w_s, 0)
    s = jnp.einsum('hts,bcshd->bcthd', w, vp) + b_s.T[None, None, :, :, None]
    s = s.reshape(B, n_chunk * CHUNK, G_HEADS, G_HEAD_DIM)[:, :T]
    return u * s


def mixer_output(u, v, attn, w_s, b_s, g_mix_out, w_out):
    B, T = u.shape[:2]
    g = chunk_spatial_gate(u, v, w_s, b_s).reshape(B, T, G_WIDTH)
    a = attn.reshape(B, T, M_WIDTH)
    y = jnp.concatenate([rms_norm(g, g_mix_out[:G_WIDTH]), rms_norm(a, g_mix_out[G_WIDTH:])], axis=-1)
    return y @ w_out


def moe(h, w_router, b_router, w_gate, w_up, w_down):
    B, T, D = h.shape
    hf = h.reshape(B * T, D)
    scores = jax.nn.sigmoid((hf @ w_router).astype(jnp.float32))
    sel = scores + b_router.astype(jnp.float32)
    grp_score = lax.top_k(sel.reshape(-1, N_GROUPS, EXPERTS_PER_GROUP), GROUP_SCORE_TOPK)[0].sum(-1)
    grp = jnp.argmax(grp_score, axis=-1)
    in_grp = (jnp.arange(N_EXPERTS) // EXPERTS_PER_GROUP)[None, :] == grp[:, None]
    _, idx = lax.top_k(jnp.where(in_grp, sel, -jnp.inf), TOP_K)
    w = jnp.take_along_axis(scores, idx, axis=-1)
    w = w / jnp.sum(w, axis=-1, keepdims=True)
    comb = jnp.sum(jax.nn.one_hot(idx, N_EXPERTS, dtype=jnp.float32) * w[..., None], axis=1).astype(h.dtype)
    a = jnp.einsum('nd,edf->nef', hf, w_gate)
    b = jnp.einsum('nd,edf->nef', hf, w_up)
    act = jax.nn.silu(a) * b * comb[:, :, None]
    return jnp.einsum('nef,efd->nd', act, w_down).reshape(B, T, D)


def decoder_layer(x, c, pos, attend, w_ada, b_ada, g_norm_mix, g_norm_ffn, w_in, g_v, w_s, b_s,
                  g_cq, w_uq, g_ckv, g_qk_q, g_mix_out, w_out, w_router, b_router, w_gate, w_up, w_down):
    sh1, sc1, gt1, sh2, sc2, gt2 = ada_modulation(c, w_ada, b_ada)
    h = rms_norm(x, g_norm_mix) * (1 + sc1) + sh1
    u, v, q, ckv, kr = mixer_inputs(h, pos, w_in, g_v, g_cq, w_uq, g_ckv, g_qk_q)
    x = x + gt1 * mixer_output(u, v, attend(q, ckv, kr), w_s, b_s, g_mix_out, w_out)
    h = rms_norm(x, g_norm_ffn) * (1 + sc2) + sh2
    x = x + gt2 * moe(h, w_router, b_router, w_gate, w_up, w_down)
    return x, v, ckv, kr


def setup_inputs(seed: int = 0) -> dict:
    key = jax.random.key(seed)
    ks = jax.random.split(key, 28)
    f32 = jnp.float32

    def nrm(k, shape, scale):
        return jax.random.normal(k, shape, f32) * scale

    def gain(k, shape):
        return 1.0 + 0.1 * jax.random.normal(k, shape, f32)

    n_pages = PAST_LEN // PAGE_SIZE
    n_used = DEC_BATCH * n_pages
    n_phys = n_used + max(1, n_used // 4)
    page_table = jax.random.permutation(ks[6], n_phys)[:n_used].reshape(DEC_BATCH, n_pages).astype(jnp.int32)
    return {
        'x_prompt': nrm(ks[0], (BATCH, SEQ, D_MODEL), 1.0),
        'x_sample': nrm(ks[1], (DEC_BATCH, DEC_SEQ, D_MODEL), 1.0),
        'c_prompt': nrm(ks[2], (BATCH, D_MODEL), 1.0),
        'c_sample': nrm(ks[3], (DEC_BATCH, D_MODEL), 1.0),
        'cache_kv_latent': nrm(ks[4], (DEPTH, n_phys, PAGE_SIZE, KV_RANK), 1.0),
        'cache_k_rope': nrm(ks[5], (DEPTH, n_phys, PAGE_SIZE, QK_ROPE), 1.0),
        'page_table': page_table,
        'w_ada': nrm(ks[7], (DEPTH, D_MODEL, N_MOD * D_MODEL), 0.5 * D_MODEL ** -0.5),
        'b_ada': nrm(ks[8], (DEPTH, N_MOD * D_MODEL), 0.02),
        'g_norm_mix': gain(ks[9], (DEPTH, D_MODEL)),
        'g_norm_ffn': gain(ks[10], (DEPTH, D_MODEL)),
        'w_in': nrm(ks[11], (DEPTH, D_MODEL, IN_WIDTH), D_MODEL ** -0.5),
        'g_v': gain(ks[12], (DEPTH, G_HEAD_DIM)),
        'w_s': nrm(ks[13], (DEPTH, G_HEADS, CHUNK, CHUNK), CHUNK ** -0.5),
        'b_s': gain(ks[14], (DEPTH, G_HEADS, CHUNK)),
        'g_cq': gain(ks[15], (DEPTH, Q_RANK)),
        'w_uq': nrm(ks[16], (DEPTH, Q_RANK, M_HEADS * QK_HEAD), Q_RANK ** -0.5),
        'g_ckv': gain(ks[17], (DEPTH, KV_RANK)),
        'w_ukv': nrm(ks[18], (DEPTH, KV_RANK, M_HEADS * (QK_NOPE + V_HEAD)), KV_RANK ** -0.5),
        'g_qk_q': gain(ks[19], (DEPTH, QK_HEAD)),
        'g_qk_k': gain(ks[20], (DEPTH, QK_HEAD)),
        'g_mix_out': gain(ks[21], (DEPTH, MIX_WIDTH)),
        'w_out': nrm(ks[22], (DEPTH, MIX_WIDTH, D_MODEL), MIX_WIDTH ** -0.5),
        'w_router': nrm(ks[23], (D_MODEL, N_EXPERTS), D_MODEL ** -0.5),
        'b_router': nrm(ks[24], (N_EXPERTS,), 0.01),
        'w_gate': nrm(ks[25], (DEPTH, N_EXPERTS, D_MODEL, D_EXPERT), D_MODEL ** -0.5),
        'w_up': nrm(ks[26], (DEPTH, N_EXPERTS, D_MODEL, D_EXPERT), D_MODEL ** -0.5),
        'w_down': nrm(ks[27], (DEPTH, N_EXPERTS, D_EXPERT, D_MODEL), D_EXPERT ** -0.5),
    }


def reference(x_prompt, x_sample, c_prompt, c_sample, cache_kv_latent, cache_k_rope, page_table,
              w_ada, b_ada, g_norm_mix, g_norm_ffn, w_in, g_v, w_s, b_s, g_cq, w_uq, g_ckv, w_ukv,
              g_qk_q, g_qk_k, g_mix_out, w_out, w_router, b_router, w_gate, w_up, w_down):
    t_p = x_prompt.shape[1]
    t_s = x_sample.shape[1]
    pos_p = jnp.arange(t_p, dtype=jnp.int32)
    pos_s = PAST_LEN + jnp.arange(t_s, dtype=jnp.int32)
    open_p = ((t_p - 1) // CHUNK) * CHUNK
    open_s = ((t_s - 1) // CHUNK) * CHUNK
    xp, xs = x_prompt, x_sample
    lat_p, rope_p, cv_p, lat_s, rope_s, cv_s = [], [], [], [], [], []
    for l in range(DEPTH):
        wu, gk = w_ukv[l], g_qk_k[l]

        def attend_p(q, ckv, kr):
            return prompt_attention(q, *mla_expand(ckv, kr, wu, gk))

        def attend_s(q, ckv, kr):
            return sample_attention(q, ckv, kr, cache_kv_latent, cache_k_rope, page_table, l, wu, gk)

        lw = (w_ada[l], b_ada[l], g_norm_mix[l], g_norm_ffn[l], w_in[l], g_v[l], w_s[l], b_s[l],
              g_cq[l], w_uq[l], g_ckv[l], g_qk_q[l], g_mix_out[l], w_out[l], w_router, b_router,
              w_gate[l], w_up[l], w_down[l])
        xp, v_p, ckv_p, kr_p = decoder_layer(xp, c_prompt, pos_p, attend_p, *lw)
        xs, v_s, ckv_s, kr_s = decoder_layer(xs, c_sample, pos_s, attend_s, *lw)
        lat_p.append(ckv_p)
        rope_p.append(kr_p)
        cv_p.append(v_p[:, open_p:])
        lat_s.append(ckv_s)
        rope_s.append(kr_s)
        cv_s.append(v_s[:, open_s:])
    return (xp, xs, jnp.stack(lat_p), jnp.stack(rope_p), jnp.stack(cv_p),
            jnp.stack(lat_s), jnp.stack(rope_s), jnp.stack(cv_s))
```

```python
import functools

import jax
import jax.numpy as jnp
from jax import lax
from jax.experimental import pallas as pl
from jax.experimental.pallas import tpu as pltpu

F32 = jnp.float32
BF16 = jnp.bfloat16

M_HEADS = 8
QK_NOPE = 64
N_GROUPS = 4
N_MOD = 6
ROPE_THETA = 10000.0
EPS = 1e-6
LANE = 128
VMEM_LIMIT = 56 * 1024 * 1024


def _cparams(sem):
    return pltpu.CompilerParams(dimension_semantics=sem, vmem_limit_bytes=VMEM_LIMIT)


def _dot(a, b):
    return jnp.dot(a, b, preferred_element_type=F32)


def _dot_nt(a, b, precision=None):
    return lax.dot_general(a, b, (((1,), (1,)), ((), ())), preferred_element_type=F32,
                           precision=precision)


def _rms(x, g, n=None):
    n = x.shape[-1] if n is None else n
    ms = jnp.sum(x * x, axis=-1, keepdims=True) * (1.0 / n)
    return x * lax.rsqrt(ms + EPS) * g


def _ada_kernel(c_ref, w_ref, b_ref, o_ref):
    c = c_ref[...]
    s = c * jax.nn.sigmoid(c)
    o_ref[0] = _dot(s.astype(BF16), w_ref[0].astype(BF16)) + b_ref[0]


def _ada(c_all, w_ada, b_ada):
    depth, d, n6 = w_ada.shape
    rows = c_all.shape[0]
    tn = 1536
    return pl.pallas_call(
        _ada_kernel,
        out_shape=jax.ShapeDtypeStruct((depth, rows, n6), F32),
        grid=(depth, n6 // tn),
        in_specs=[pl.BlockSpec((rows, d), lambda l, j: (0, 0)),
                  pl.BlockSpec((1, d, tn), lambda l, j: (l, 0, j)),
                  pl.BlockSpec((1, 1, tn), lambda l, j: (l, 0, j))],
        out_specs=pl.BlockSpec((1, rows, tn), lambda l, j: (l, 0, j)),
        compiler_params=_cparams(("arbitrary", "arbitrary")),
        name="ada_modulation",
    )(c_all, w_ada, b_ada.reshape(depth, 1, n6))


def _mixer_in_kernel(dims, sample, *refs):
    gw, qrank, kvrank, gheads, vw = dims
    (x_ref, sh_ref, sc_ref, gnm_ref, win_ref, gv_ref, gcq_ref, wuq_ref, gckv_ref, wkv_ref,
     gq_ref, gk_ref, cq_ref, sq_ref, gmix_ref) = refs[:15]
    if sample:
        w00_ref, b0_ref, wukt_ref = refs[15:18]
        gn_ref, v_ref, q_ref, ckv_ref, kr_ref, k_ref, qa_ref = refs[18:]
    else:
        ws_ref, bs_ref = refs[15:17]
        gn_ref, v_ref, q_ref, ckv_ref, kr_ref, k_ref, vv_ref = refs[17:]

    x = x_ref[...]
    h = _rms(x, gnm_ref[...]) * (1.0 + sc_ref[...]) + sh_ref[...]
    z = _dot(h.astype(BF16), win_ref[...])
    tm = z.shape[0]
    u = jax.nn.gelu(z[:, :gw])
    vg = jax.nn.gelu(z[:, gw:2 * gw])
    hd = gw // gheads
    gv = gv_ref[...]
    v = jnp.concatenate([_rms(vg[:, i * hd:(i + 1) * hd], gv) for i in range(gheads)], axis=-1)
    v_ref[...] = v

    if sample:
        s = v * w00_ref[...] + b0_ref[...]
    else:
        chunk = ws_ref.shape[-1]
        row = lax.broadcasted_iota(jnp.int32, (chunk, chunk), 0)
        col = lax.broadcasted_iota(jnp.int32, (chunk, chunk), 1)
        vb = v.astype(BF16)
        cols = []
        for i in range(gheads):
            wt = jnp.where(col <= row, ws_ref[i], 0.0).astype(BF16)
            rows = [_dot(wt, vb[c * chunk:(c + 1) * chunk, i * hd:(i + 1) * hd]) + bs_ref[i]
                    for c in range(tm // chunk)]
            cols.append(jnp.concatenate(rows, axis=0) if len(rows) > 1 else rows[0])
        s = jnp.concatenate(cols, axis=-1)
    g = u * s
    gn_ref[...] = _rms(g, gmix_ref[...]).astype(gn_ref.dtype)

    o0 = 2 * gw
    cq = _rms(z[:, o0:o0 + qrank], gcq_ref[...])
    qq = _dot(cq.astype(BF16), wuq_ref[...])
    hw = M_HEADS * LANE
    cqt = cq_ref[...]
    sqt = sq_ref[...]
    gq = gq_ref[...]
    n_real = QK_NOPE + kr_ref.shape[-1]
    q_heads = []
    for i in range(M_HEADS):
        qh = qq[:, i * LANE:(i + 1) * LANE] * cqt + qq[:, hw + i * LANE:hw + (i + 1) * LANE] * sqt
        q_heads.append(_rms(qh, gq, n_real))
    q = jnp.concatenate(q_heads, axis=-1)
    q_ref[...] = q.astype(q_ref.dtype)

    o1 = o0 + qrank
    ckv = _rms(z[:, o1:o1 + kvrank], gckv_ref[...])
    ckv_ref[...] = ckv
    o2 = o1 + kvrank
    krp = z[:, o2:o2 + LANE] * cqt + z[:, o2 + LANE:o2 + 2 * LANE] * sqt
    rope = kr_ref.shape[-1]
    kr_ref[...] = krp[:, QK_NOPE:QK_NOPE + rope]
    kv = _dot(ckv.astype(BF16), wkv_ref[...])
    kr_ss = jnp.sum(krp * krp, axis=-1, keepdims=True)
    gk = gk_ref[...]
    k_heads = []
    for i in range(M_HEADS):
        kn = kv[:, i * LANE:(i + 1) * LANE]
        ss = jnp.sum(kn * kn, axis=-1, keepdims=True) + kr_ss
        k_heads.append((kn + krp) * lax.rsqrt(ss * (1.0 / n_real) + EPS) * gk)
    k_ref[...] = jnp.concatenate(k_heads, axis=-1).astype(k_ref.dtype)
    if sample:
        lane = lax.broadcasted_iota(jnp.int32, (1, LANE), 1)
        gk_nope = jnp.where(lane < QK_NOPE, gk, 0.0)
        qa = [_dot((q_heads[i] * gk_nope).astype(BF16), wukt_ref[i]) for i in range(M_HEADS)]
        qa_ref[...] = jnp.concatenate(qa, axis=-1)
    else:
        vv_ref[...] = kv[:, hw:hw + vw].astype(vv_ref.dtype)


def _mixer_in(x, mods, lw, tabs, *, sample, t_len, tm, rope):
    n, d = x.shape
    (g_norm_mix, w_in_p, g_v, g_cq, w_uq_p, g_ckv, w_kv_p, gq_p, gk_p, g_mix_g, extra) = lw
    cq_tab, sq_tab = tabs
    gw = g_mix_g.shape[-1]
    qrank = g_cq.shape[-1]
    kvrank = g_ckv.shape[-1]
    gheads = gw // g_v.shape[-1]
    assert w_in_p.shape[1] == 2 * gw + qrank + kvrank + 2 * LANE
    vw = w_kv_p.shape[1] - M_HEADS * LANE
    nt = n // tm
    full = lambda a: pl.BlockSpec(a.shape, lambda i: (0,) * a.ndim)
    if sample:
        mod_spec = lambda k: pl.BlockSpec((tm, d), lambda i, k=k: (i, k))
        tab_spec = pl.BlockSpec((1, LANE), lambda i: (0, 0))
    else:
        tpb = t_len // tm
        mod_spec = lambda k: pl.BlockSpec((None, 1, d), lambda i, k=k: (i // tpb, 0, k))
        tab_spec = pl.BlockSpec((tm, LANE), lambda i: (i % tpb, 0))
    row = lambda w: pl.BlockSpec((tm, w), lambda i: (i, 0))
    in_specs = [row(d), mod_spec(0), mod_spec(1), full(g_norm_mix), full(w_in_p), full(g_v), full(g_cq),
                full(w_uq_p), full(g_ckv), full(w_kv_p), full(gq_p), full(gk_p), tab_spec, tab_spec,
                full(g_mix_g)] + [full(a) for a in extra]
    kr_w = rope
    hw = M_HEADS * LANE
    if sample:
        out_shape = [jax.ShapeDtypeStruct((n, gw), BF16), jax.ShapeDtypeStruct((n, gw), F32),
                     jax.ShapeDtypeStruct((n, hw), F32), jax.ShapeDtypeStruct((n, kvrank), F32),
                     jax.ShapeDtypeStruct((n, kr_w), F32), jax.ShapeDtypeStruct((n, hw), F32),
                     jax.ShapeDtypeStruct((n, M_HEADS * kvrank), F32)]
        out_specs = [row(gw), row(gw), row(hw), row(kvrank), row(kr_w), row(hw), row(M_HEADS * kvrank)]
    else:
        out_shape = [jax.ShapeDtypeStruct((n, gw), BF16), jax.ShapeDtypeStruct((n, gw), F32),
                     jax.ShapeDtypeStruct((n, hw), BF16), jax.ShapeDtypeStruct((n, kvrank), F32),
                     jax.ShapeDtypeStruct((n, kr_w), F32), jax.ShapeDtypeStruct((n, hw), BF16),
                     jax.ShapeDtypeStruct((n, vw), BF16)]
        out_specs = [row(gw), row(gw), row(hw), row(kvrank), row(kr_w), row(hw), row(vw)]
    dims = (gw, qrank, kvrank, gheads, vw)
    return pl.pallas_call(
        functools.partial(_mixer_in_kernel, dims, sample),
        out_shape=out_shape,
        grid=(nt,),
        in_specs=in_specs,
        out_specs=out_specs,
        compiler_params=_cparams(("arbitrary",)),
        name="mixer_in_sample" if sample else "mixer_in_prompt",
    )(x, mods, mods, g_norm_mix, w_in_p, g_v, g_cq, w_uq_p, g_ckv, w_kv_p, gq_p, gk_p, cq_tab, sq_tab,
      g_mix_g, *extra)


def _prompt_attn_kernel(tq, vhead, q_ref, k_ref, v_ref, o_ref):
    qi = pl.program_id(2)
    heads = q_ref.shape[-1] // LANE
    row = lax.broadcasted_iota(jnp.int32, (tq, tq), 0)
    col = lax.broadcasted_iota(jnp.int32, (tq, tq), 1)
    outs = []
    for hh in range(heads):
        q = q_ref[:, hh * LANE:(hh + 1) * LANE]

        def kv_block(j):
            start = pl.multiple_of(j * tq, tq)
            k = k_ref[pl.ds(start, tq), hh * LANE:(hh + 1) * LANE]
            v = v_ref[pl.ds(start, tq), hh * vhead:(hh + 1) * vhead]
            return _dot_nt(q, k), v

        def update(carry, s, v):
            m, l, acc = carry
            m_new = jnp.maximum(m, jnp.max(s, axis=-1, keepdims=True))
            alpha = jnp.exp(m - m_new)
            p = jnp.exp(s - m_new)
            l = l * alpha + jnp.sum(p, axis=-1, keepdims=True)
            acc = acc * alpha + _dot(p.astype(BF16), v)
            return m_new, l, acc

        def body(j, carry):
            s, v = kv_block(j)
            return update(carry, s, v)

        init = (jnp.full((tq, 1), -jnp.inf, F32), jnp.zeros((tq, 1), F32), jnp.zeros((tq, vhead), F32))
        carry = lax.fori_loop(0, qi, body, init)
        s, v = kv_block(qi)
        m, l, acc = update(carry, jnp.where(col <= row, s, -jnp.inf), v)
        outs.append(acc / l)
    o_ref[...] = jnp.concatenate(outs, axis=-1).astype(o_ref.dtype)


def _prompt_attn(q, k, v, *, batch, t_len, tq):
    n = q.shape[0]
    vhead = v.shape[1] // M_HEADS
    hp = LANE // vhead
    nq = t_len // tq
    return pl.pallas_call(
        functools.partial(_prompt_attn_kernel, tq, vhead),
        out_shape=jax.ShapeDtypeStruct((n, v.shape[1]), F32),
        grid=(batch, M_HEADS // hp, nq),
        in_specs=[pl.BlockSpec((tq, hp * LANE), lambda b, h, i: (b * nq + i, h)),
                  pl.BlockSpec((t_len, hp * LANE), lambda b, h, i: (b, h)),
                  pl.BlockSpec((t_len, hp * vhead), lambda b, h, i: (b, h))],
        out_specs=pl.BlockSpec((tq, hp * vhead), lambda b, h, i: (b * nq + i, h)),
        compiler_params=_cparams(("arbitrary", "arbitrary", "arbitrary")),
        name="prompt_attention",
    )(q, k, v)


def _sample_attn_kernel(npg, rope, pt_ref, q_ref, qa_ref, knew_ref, cnew_ref, gk_ref, wukt_ref, wuv_ref,
                        *refs):
    c_refs = refs[:npg]
    kr_refs = refs[npg:2 * npg]
    o_ref = refs[2 * npg]
    m_ref, l_ref, acc_ref = refs[2 * npg + 1:]
    j = pl.program_id(1)
    nj = pl.num_programs(1)

    @pl.when(j == 0)
    def _():
        m_ref[...] = jnp.full(m_ref.shape, -jnp.inf, F32)
        l_ref[...] = jnp.zeros(l_ref.shape, F32)
        acc_ref[...] = jnp.zeros(acc_ref.shape, F32)

    q = q_ref[0]
    qa = qa_ref[0].astype(BF16)
    qr = (q * gk_ref[...])[:, QK_NOPE:QK_NOPE + rope].astype(BF16)
    n_real = QK_NOPE + rope
    eye = (lax.broadcasted_iota(jnp.int32, (rope, rope), 0)
           == lax.broadcasted_iota(jnp.int32, (rope, rope), 1)).astype(F32)
    wukt = wukt_ref[...]
    s_parts, c_parts = [], []
    for i in range(npg):
        c = c_refs[i][...].astype(BF16)
        kr = kr_refs[i][...]
        kt = _dot_nt(wukt, c)
        page = kt.shape[-1]
        ss = jnp.sum((kt * kt).reshape(M_HEADS, QK_NOPE, page), axis=1)
        krt = _dot_nt(eye, kr, precision=lax.Precision.HIGHEST)
        kr_ss = jnp.sum(krt * krt, axis=0, keepdims=True)
        rinv = lax.rsqrt((ss + kr_ss) * (1.0 / n_real) + EPS)
        sc = _dot_nt(qa, c) + _dot_nt(qr, kr.astype(BF16))
        s_parts.append(sc * rinv)
        c_parts.append(c)
    s = jnp.concatenate(s_parts, axis=-1) if npg > 1 else s_parts[0]
    m_old = m_ref[...]
    m_new = jnp.maximum(m_old, jnp.max(s, axis=-1, keepdims=True))
    alpha = jnp.exp(m_old - m_new)
    p = jnp.exp(s - m_new)
    l_ref[...] = l_ref[...] * alpha + jnp.sum(p, axis=-1, keepdims=True)
    pb = p.astype(BF16)
    page = c_parts[0].shape[0]
    pv = _dot(pb[:, :page], c_parts[0])
    for i in range(1, npg):
        pv = pv + _dot(pb[:, i * page:(i + 1) * page], c_parts[i])
    acc_ref[...] = acc_ref[...] * alpha + pv
    m_ref[...] = m_new

    @pl.when(j == nj - 1)
    def _():
        s_new = jnp.sum(q * knew_ref[0], axis=-1, keepdims=True)
        m_old = m_ref[...]
        m_fin = jnp.maximum(m_old, s_new)
        alpha = jnp.exp(m_old - m_fin)
        p_new = jnp.exp(s_new - m_fin)
        l = l_ref[...] * alpha + p_new
        c_new = cnew_ref[0].astype(BF16).astype(F32)
        o_lat = (acc_ref[...] * alpha + p_new.astype(BF16).astype(F32) * c_new) / l
        full = _dot(o_lat.astype(BF16), wuv_ref[...])
        vhead = full.shape[-1] // M_HEADS
        hrow = lax.broadcasted_iota(jnp.int32, full.shape, 0)
        hcol = lax.broadcasted_iota(jnp.int32, full.shape, 1) // vhead
        o_ref[0] = jnp.sum(jnp.where(hrow == hcol, full, 0.0), axis=0, keepdims=True)


def _sample_attn(page_table, q, qa, k_new, c_new, gk_p, wukt, wuv, cache_kv, cache_kr, *, layer):
    s_n = q.shape[0]
    n_pages = page_table.shape[1]
    page = cache_kv.shape[2]
    kvrank = cache_kv.shape[3]
    rope = cache_kr.shape[3]
    npg = 8
    while n_pages % npg:
        npg //= 2
    nj = n_pages // npg
    vw = wuv.shape[1]
    pt_flat = page_table.reshape(-1)

    def page_spec(width, i):
        return pl.BlockSpec((None, None, page, width),
                            lambda s, j, pt, i=i: (layer, pt[s * n_pages + j * npg + i], 0, 0))

    seq3 = lambda a, b: pl.BlockSpec((1, a, b), lambda s, j, pt: (s, 0, 0))
    full = lambda a: pl.BlockSpec(a.shape, lambda s, j, pt: (0,) * a.ndim)
    in_specs = ([seq3(M_HEADS, LANE), seq3(M_HEADS, kvrank), seq3(M_HEADS, LANE), seq3(1, kvrank),
                 full(gk_p), full(wukt), full(wuv)]
                + [page_spec(kvrank, i) for i in range(npg)] + [page_spec(rope, i) for i in range(npg)])
    return pl.pallas_call(
        functools.partial(_sample_attn_kernel, npg, rope),
        out_shape=jax.ShapeDtypeStruct((s_n, 1, vw), F32),
        grid_spec=pltpu.PrefetchScalarGridSpec(
            num_scalar_prefetch=1, grid=(s_n, nj), in_specs=in_specs,
            out_specs=pl.BlockSpec((1, 1, vw), lambda s, j, pt: (s, 0, 0)),
            scratch_shapes=[pltpu.VMEM((M_HEADS, 1), F32), pltpu.VMEM((M_HEADS, 1), F32),
                            pltpu.VMEM((M_HEADS, kvrank), F32)]),
        compiler_params=_cparams(("arbitrary", "arbitrary")),
        name="sample_attention",
    )(pt_flat, q.reshape(s_n, M_HEADS, LANE), qa.reshape(s_n, M_HEADS, kvrank),
      k_new.reshape(s_n, M_HEADS, LANE), c_new.reshape(s_n, 1, kvrank), gk_p, wukt, wuv,
      *([cache_kv] * npg), *([cache_kr] * npg)).reshape(s_n, vw)


def _mixer_out_kernel(n_exp, gn_ref, a_ref, x_ref, gt_ref, sh_ref, sc_ref, gmix_ref, wout_ref, gffn_ref,
                      wr_ref, br_ref, x1_ref, h2_ref, comb_ref):
    gw = gn_ref.shape[-1]
    an = _rms(a_ref[...].astype(F32), gmix_ref[...])
    y = _dot(gn_ref[...], wout_ref[:gw, :]) + _dot(an.astype(BF16), wout_ref[gw:, :])
    x1 = x_ref[...] + gt_ref[...] * y
    x1_ref[...] = x1
    h2 = _rms(x1, gffn_ref[...]) * (1.0 + sc_ref[...]) + sh_ref[...]
    h2_ref[...] = h2.astype(h2_ref.dtype)
    tm = h2.shape[0]

    logits = _dot_nt(wr_ref[...], h2, precision=lax.Precision.HIGHEST)
    scores = jax.nn.sigmoid(logits)
    sel = scores + br_ref[...]
    per = n_exp // N_GROUPS
    best = None
    for g in range(N_GROUPS):
        a, b, c, d = [sel[g * per + i:g * per + i + 1, :] for i in range(per)]
        hi1, lo1 = jnp.maximum(a, b), jnp.minimum(a, b)
        hi2, lo2 = jnp.maximum(c, d), jnp.minimum(c, d)
        gs = jnp.maximum(hi1, hi2) + jnp.maximum(jnp.minimum(hi1, hi2), jnp.maximum(lo1, lo2))
        if best is None:
            best, grp = gs, jnp.zeros(gs.shape, jnp.int32)
        else:
            better = gs > best
            grp = jnp.where(better, g, grp)
            best = jnp.where(better, gs, best)
    erow = lax.broadcasted_iota(jnp.int32, (n_exp, tm), 0)
    selm = jnp.where(erow // per == grp, sel, -jnp.inf)
    m1 = jnp.max(selm, axis=0, keepdims=True)
    i1 = jnp.min(jnp.where(selm == m1, erow, n_exp), axis=0, keepdims=True)
    oh1 = erow == i1
    selm2 = jnp.where(oh1, -jnp.inf, selm)
    m2 = jnp.max(selm2, axis=0, keepdims=True)
    i2 = jnp.min(jnp.where(selm2 == m2, erow, n_exp), axis=0, keepdims=True)
    oh2 = erow == i2
    s1 = jnp.sum(jnp.where(oh1, scores, 0.0), axis=0, keepdims=True)
    s2 = jnp.sum(jnp.where(oh2, scores, 0.0), axis=0, keepdims=True)
    tot = s1 + s2
    comb_t = jnp.where(oh1, s1 / tot, 0.0) + jnp.where(oh2, s2 / tot, 0.0)
    comb_pad = jnp.concatenate([comb_t, jnp.zeros((LANE - n_exp, tm), F32)], axis=0)
    comb_ref[...] = comb_pad.T


def _mixer_out(gn, attn, x, mods, g_mix_a, w_out_b, g_norm_ffn, w_router_t, b_router, *, sample, t_len, tm):
    n, d = x.shape
    gw = gn.shape[1]
    n_exp = w_router_t.shape[0]
    full = lambda a: pl.BlockSpec(a.shape, lambda i: (0,) * a.ndim)
    row = lambda w: pl.BlockSpec((tm, w), lambda i: (i, 0))
    if sample:
        mod_spec = lambda k: pl.BlockSpec((tm, d), lambda i, k=k: (i, k))
    else:
        tpb = t_len // tm
        mod_spec = lambda k: pl.BlockSpec((None, 1, d), lambda i, k=k: (i // tpb, 0, k))
    return pl.pallas_call(
        functools.partial(_mixer_out_kernel, n_exp),
        out_shape=[jax.ShapeDtypeStruct((n, d), F32), jax.ShapeDtypeStruct((n, d), BF16),
                   jax.ShapeDtypeStruct((n, LANE), F32)],
        grid=(n // tm,),
        in_specs=[row(gw), row(attn.shape[1]), row(d), mod_spec(2), mod_spec(3), mod_spec(4), full(g_mix_a),
                  full(w_out_b), full(g_norm_ffn), full(w_router_t), full(b_router)],
        out_specs=[row(d), row(d), row(LANE)],
        compiler_params=_cparams(("arbitrary",)),
        name="mixer_out_sample" if sample else "mixer_out_prompt",
    )(gn, attn, x, mods, mods, mods, g_mix_a, w_out_b, g_norm_ffn, w_router_t, b_router)


def _moe_kernel(h_ref, comb_ref, x1_ref, gt_ref, wg_ref, wu_ref, wd_ref, o_ref, acc_ref):
    e = pl.program_id(1)

    @pl.when(e == 0)
    def _():
        acc_ref[...] = jnp.zeros(acc_ref.shape, F32)

    h = h_ref[...]
    comb = comb_ref[...]
    lane = lax.broadcasted_iota(jnp.int32, comb.shape, 1)
    ce = jnp.sum(jnp.where(lane == e, comb, 0.0), axis=-1, keepdims=True)
    a = _dot(h, wg_ref[...].astype(BF16))
    b = _dot(h, wu_ref[...].astype(BF16))
    act = (a * jax.nn.sigmoid(a)) * b * ce
    acc_ref[...] += _dot(act.astype(BF16), wd_ref[...].astype(BF16))

    @pl.when(e == pl.num_programs(1) - 1)
    def _():
        o_ref[...] = x1_ref[...] + gt_ref[...] * acc_ref[...]


def _moe(h2, comb, x1, mods, w_gate, w_up, w_down, *, layer, sample, t_len, tm):
    n, d = x1.shape
    n_exp, _, f = w_gate.shape[1:]
    row = lambda w: pl.BlockSpec((tm, w), lambda i, e: (i, 0))
    if sample:
        mod_spec = pl.BlockSpec((tm, d), lambda i, e: (i, 5))
    else:
        tpb = t_len // tm
        mod_spec = pl.BlockSpec((None, 1, d), lambda i, e: (i // tpb, 0, 5))
    return pl.pallas_call(
        _moe_kernel,
        out_shape=jax.ShapeDtypeStruct((n, d), F32),
        grid=(n // tm, n_exp),
        in_specs=[row(d), row(LANE), row(d), mod_spec,
                  pl.BlockSpec((None, None, d, f), lambda i, e: (layer, e, 0, 0)),
                  pl.BlockSpec((None, None, d, f), lambda i, e: (layer, e, 0, 0)),
                  pl.BlockSpec((None, None, f, d), lambda i, e: (layer, e, 0, 0))],
        out_specs=row(d),
        scratch_shapes=[pltpu.VMEM((tm, d), F32)],
        compiler_params=_cparams(("arbitrary", "arbitrary")),
        name="moe_sample" if sample else "moe_prompt",
    )(h2, comb, x1, mods, w_gate, w_up, w_down)


def _rope_tables(pos, rope):
    half = rope // 2
    freqs = ROPE_THETA ** (-jnp.arange(half, dtype=F32) / half)
    ang = pos.astype(F32)[:, None] * freqs[None, :]
    cos, sin = jnp.cos(ang), jnp.sin(ang)
    t = pos.shape[0]
    pad = LANE - QK_NOPE - rope
    cq = jnp.concatenate([jnp.ones((t, QK_NOPE), F32), cos, cos, jnp.zeros((t, pad), F32)], axis=-1)
    sq = jnp.concatenate([jnp.zeros((t, QK_NOPE), F32), sin, sin, jnp.zeros((t, pad), F32)], axis=-1)
    return cq, sq


def _rot_cols(w, rope):
    half = rope // 2
    return jnp.concatenate([-w[..., half:], w[..., :half]], axis=-1)


def _prep_layer(l, w_in, w_uq, w_ukv, g_qk_q, g_qk_k, rope, gw, qrank, kvrank, vhead):
    d = w_in.shape[1]
    qk_head = QK_NOPE + rope
    pad = LANE - qk_head
    o = 2 * gw + qrank + kvrank
    w_kr = w_in[l][:, o:o + rope]
    z64 = jnp.zeros((d, QK_NOPE), F32)
    zp = jnp.zeros((d, pad), F32)
    w_in_p = jnp.concatenate([w_in[l][:, :o], z64, w_kr, zp, z64, _rot_cols(w_kr, rope), zp], axis=-1).astype(BF16)
    wq = w_uq[l].reshape(qrank, M_HEADS, qk_head)
    zq = jnp.zeros((qrank, M_HEADS, pad), F32)
    wq_plain = jnp.concatenate([wq, zq], axis=-1).reshape(qrank, M_HEADS * LANE)
    wq_rot = jnp.concatenate([jnp.zeros((qrank, M_HEADS, QK_NOPE), F32), _rot_cols(wq[..., QK_NOPE:], rope), zq],
                             axis=-1).reshape(qrank, M_HEADS * LANE)
    w_uq_p = jnp.concatenate([wq_plain, wq_rot], axis=-1).astype(BF16)
    wkv = w_ukv[l].reshape(kvrank, M_HEADS, QK_NOPE + vhead)
    wk = wkv[..., :QK_NOPE]
    wk_pad = jnp.concatenate([wk, jnp.zeros((kvrank, M_HEADS, LANE - QK_NOPE), F32)], axis=-1)
    wv = wkv[..., QK_NOPE:].reshape(kvrank, M_HEADS * vhead)
    w_kv_p = jnp.concatenate([wk_pad.reshape(kvrank, M_HEADS * LANE), wv], axis=-1).astype(BF16)
    wukt_pad = jnp.concatenate([wk.transpose(1, 2, 0), jnp.zeros((M_HEADS, LANE - QK_NOPE, kvrank), F32)],
                               axis=1).astype(BF16)
    wukt = wk.transpose(1, 2, 0).reshape(M_HEADS * QK_NOPE, kvrank).astype(BF16)
    zg = jnp.zeros((pad,), F32)
    gq_p = (jnp.concatenate([g_qk_q[l], zg]) * (qk_head ** -0.5)).reshape(1, LANE)
    gk_p = jnp.concatenate([g_qk_k[l], zg]).reshape(1, LANE)
    return w_in_p, w_uq_p, w_kv_p, wukt_pad, wukt, wv.astype(BF16), gq_p, gk_p


def kernel(x_prompt, x_sample, c_prompt, c_sample, cache_kv_latent, cache_k_rope, page_table, w_ada, b_ada,
           g_norm_mix, g_norm_ffn, w_in, g_v, w_s, b_s, g_cq, w_uq, g_ckv, w_ukv, g_qk_q, g_qk_k, g_mix_out,
           w_out, w_router, b_router, w_gate, w_up, w_down):
    batch, t_p, d = x_prompt.shape
    s_n, t_s, _ = x_sample.shape
    depth = w_ada.shape[0]
    gheads, chunk = w_s.shape[1], w_s.shape[2]
    hd = g_v.shape[-1]
    gw = gheads * hd
    qrank = g_cq.shape[-1]
    kvrank = g_ckv.shape[-1]
    rope = cache_k_rope.shape[-1]
    vhead = w_ukv.shape[-1] // M_HEADS - QK_NOPE
    past = page_table.shape[1] * cache_kv_latent.shape[2]
    assert t_s == 1 and t_p % chunk == 0 and g_qk_q.shape[-1] == QK_NOPE + rope
    row2 = lambda a: a.reshape(1, -1)

    mods = _ada(jnp.concatenate([c_prompt, c_sample], axis=0), w_ada, b_ada)
    tab_p = _rope_tables(jnp.arange(t_p, dtype=jnp.int32), rope)
    tab_s = _rope_tables(past + jnp.arange(t_s, dtype=jnp.int32), rope)
    w_router_t = w_router.T
    b_router_c = b_router.reshape(-1, 1)

    tm_p = 256
    tm_moe = 1024 if t_p % 1024 == 0 else tm_p
    tm_s = s_n
    tq = 256
    xp = x_prompt.reshape(batch * t_p, d)
    xs = x_sample.reshape(s_n, d)
    open_p = ((t_p - 1) // chunk) * chunk
    outs = [[] for _ in range(6)]
    for l in range(depth):
        w_in_p, w_uq_p, w_kv_p, wukt_pad, wukt, wuv, gq_p, gk_p = _prep_layer(
            l, w_in, w_uq, w_ukv, g_qk_q, g_qk_k, rope, gw, qrank, kvrank, vhead)
        g_mix_g, g_mix_a = row2(g_mix_out[l, :gw]), row2(g_mix_out[l, gw:])
        w_out_b = w_out[l].astype(BF16)
        base = (row2(g_norm_mix[l]), w_in_p, row2(g_v[l]), row2(g_cq[l]), w_uq_p, row2(g_ckv[l]), w_kv_p,
                gq_p, gk_p, g_mix_g)
        mp = mods[l, :batch].reshape(batch, 1, N_MOD * d)
        ms = mods[l, batch:]

        bs_full = jnp.broadcast_to(b_s[l][:, :, None], (gheads, chunk, hd))
        gn, v, q, ckv, kr, k, vv = _mixer_in(xp, mp, base + ((w_s[l], bs_full),), tab_p,
                                             sample=False, t_len=t_p, tm=tm_p, rope=rope)
        attn = _prompt_attn(q, k, vv, batch=batch, t_len=t_p, tq=tq)
        x1, h2, comb = _mixer_out(gn, attn, xp, mp, g_mix_a, w_out_b, row2(g_norm_ffn[l]), w_router_t,
                                  b_router_c, sample=False, t_len=t_p, tm=tm_p)
        xp = _moe(h2, comb, x1, mp, w_gate, w_up, w_down, layer=l, sample=False, t_len=t_p, tm=tm_moe)
        outs[0].append(ckv.reshape(batch, t_p, kvrank))
        outs[1].append(kr.reshape(batch, t_p, rope))
        outs[2].append(v.reshape(batch, t_p, gheads, hd)[:, open_p:])

        w00 = row2(jnp.repeat(w_s[l, :, 0, 0], hd))
        b0 = row2(jnp.repeat(b_s[l, :, 0], hd))
        gn, v, q, ckv, kr, k, qa = _mixer_in(xs, ms, base + ((w00, b0, wukt_pad),), tab_s,
                                             sample=True, t_len=t_s, tm=tm_s, rope=rope)
        attn = _sample_attn(page_table, q, qa, k, ckv, gk_p, wukt, wuv, cache_kv_latent, cache_k_rope, layer=l)
        x1, h2, comb = _mixer_out(gn, attn, xs, ms, g_mix_a, w_out_b, row2(g_norm_ffn[l]), w_router_t,
                                  b_router_c, sample=True, t_len=t_s, tm=tm_s)
        xs = _moe(h2, comb, x1, ms, w_gate, w_up, w_down, layer=l, sample=True, t_len=t_s, tm=tm_s)
        outs[3].append(ckv.reshape(s_n, t_s, kvrank))
        outs[4].append(kr.reshape(s_n, t_s, rope))
        outs[5].append(v.reshape(s_n, t_s, gheads, hd))

    return (xp.reshape(batch, t_p, d), xs.reshape(s_n, t_s, d), jnp.stack(outs[0]), jnp.stack(outs[1]),
            jnp.stack(outs[2]), jnp.stack(outs[3]), jnp.stack(outs[4]), jnp.stack(outs[5]))
```

```python
import functools

import jax
import jax.numpy as jnp
from jax import lax
from jax.experimental import pallas as pl
from jax.experimental.pallas import tpu as pltpu

F32 = jnp.float32
BF16 = jnp.bfloat16

M_HEADS = 8
QK_NOPE = 64
N_GROUPS = 4
N_MOD = 6
ROPE_THETA = 10000.0
EPS = 1e-6
LANE = 128
MXU_TILE = 256
VMEM_LIMIT = 56 * 1024 * 1024


def _cparams(sem):
    return pltpu.CompilerParams(dimension_semantics=sem, vmem_limit_bytes=VMEM_LIMIT)


def _dot(a, b):
    return jnp.dot(a, b, preferred_element_type=F32)


def _dot_nt(a, b, precision=None):
    return lax.dot_general(a, b, (((1,), (1,)), ((), ())), preferred_element_type=F32,
                           precision=precision)


def _rms(x, g, n=None):
    n = x.shape[-1] if n is None else n
    ms = jnp.sum(x * x, axis=-1, keepdims=True) * (1.0 / n)
    return x * lax.rsqrt(ms + EPS) * g


def _ada_kernel(c_ref, w_ref, b_ref, o_ref):
    c = c_ref[...]
    s = c * jax.nn.sigmoid(c)
    o_ref[0] = _dot(s.astype(BF16), w_ref[0].astype(BF16)) + b_ref[0]


def _ada(c_all, w_ada, b_ada):
    depth, d, n6 = w_ada.shape
    rows = c_all.shape[0]
    tn = 1536
    return pl.pallas_call(
        _ada_kernel,
        out_shape=jax.ShapeDtypeStruct((depth, rows, n6), F32),
        grid=(depth, n6 // tn),
        in_specs=[pl.BlockSpec((rows, d), lambda l, j: (0, 0)),
                  pl.BlockSpec((1, d, tn), lambda l, j: (l, 0, j)),
                  pl.BlockSpec((1, 1, tn), lambda l, j: (l, 0, j))],
        out_specs=pl.BlockSpec((1, rows, tn), lambda l, j: (l, 0, j)),
        compiler_params=_cparams(("arbitrary", "arbitrary")),
        name="ada_modulation",
    )(c_all, w_ada, b_ada.reshape(depth, 1, n6))


def _mixer_in_kernel(dims, sample, *refs):
    gw, qrank, kvrank, gheads, vw = dims
    (x_ref, sh_ref, sc_ref, gnm_ref, win_ref, gv_ref, gcq_ref, wuq_ref, gckv_ref, wkv_ref,
     gq_ref, gk_ref, cq_ref, sq_ref, gmix_ref) = refs[:15]
    if sample:
        w00_ref, b0_ref, wukt_ref = refs[15:18]
        gn_ref, v_ref, q_ref, ckv_ref, kr_ref, k_ref, qa_ref = refs[18:]
    else:
        ws_ref, bs_ref = refs[15:17]
        gn_ref, v_ref, q_ref, ckv_ref, kr_ref, k_ref, vv_ref = refs[17:]

    x = x_ref[...]
    h = _rms(x, gnm_ref[...]) * (1.0 + sc_ref[...]) + sh_ref[...]
    z = _dot(h.astype(BF16), win_ref[...])
    tm = z.shape[0]
    u = jax.nn.gelu(z[:, :gw])
    vg = jax.nn.gelu(z[:, gw:2 * gw])
    hd = gw // gheads
    gv = gv_ref[...]
    v = jnp.concatenate([_rms(vg[:, i * hd:(i + 1) * hd], gv) for i in range(gheads)], axis=-1)
    v_ref[...] = v

    if sample:
        s = v * w00_ref[...] + b0_ref[...]
    else:
        chunk = ws_ref.shape[-1]
        row = lax.broadcasted_iota(jnp.int32, (chunk, chunk), 0)
        col = lax.broadcasted_iota(jnp.int32, (chunk, chunk), 1)
        vb = v.astype(BF16)
        cols = []
        for i in range(gheads):
            wt = jnp.where(col <= row, ws_ref[i], 0.0).astype(BF16)
            rows = [_dot(wt, vb[c * chunk:(c + 1) * chunk, i * hd:(i + 1) * hd]) + bs_ref[i]
                    for c in range(tm // chunk)]
            cols.append(jnp.concatenate(rows, axis=0) if len(rows) > 1 else rows[0])
        s = jnp.concatenate(cols, axis=-1)
    g = u * s
    gn_ref[...] = _rms(g, gmix_ref[...]).astype(gn_ref.dtype)

    o0 = 2 * gw
    cq = _rms(z[:, o0:o0 + qrank], gcq_ref[...])
    qq = _dot(cq.astype(BF16), wuq_ref[...])
    hw = M_HEADS * LANE
    cqt = cq_ref[...]
    sqt = sq_ref[...]
    gq = gq_ref[...]
    n_real = QK_NOPE + kr_ref.shape[-1]
    q_heads = []
    for i in range(M_HEADS):
        qh = qq[:, i * LANE:(i + 1) * LANE] * cqt + qq[:, hw + i * LANE:hw + (i + 1) * LANE] * sqt
        q_heads.append(_rms(qh, gq, n_real))
    q = jnp.concatenate(q_heads, axis=-1)
    q_ref[...] = q.astype(q_ref.dtype)

    o1 = o0 + qrank
    ckv = _rms(z[:, o1:o1 + kvrank], gckv_ref[...])
    ckv_ref[...] = ckv
    o2 = o1 + kvrank
    krp = z[:, o2:o2 + LANE] * cqt + z[:, o2 + LANE:o2 + 2 * LANE] * sqt
    rope = kr_ref.shape[-1]
    kr_ref[...] = krp[:, QK_NOPE:QK_NOPE + rope]
    kv = _dot(ckv.astype(BF16), wkv_ref[...])
    kr_ss = jnp.sum(krp * krp, axis=-1, keepdims=True)
    gk = gk_ref[...]
    k_heads = []
    for i in range(M_HEADS):
        kn = kv[:, i * LANE:(i + 1) * LANE]
        ss = jnp.sum(kn * kn, axis=-1, keepdims=True) + kr_ss
        k_heads.append((kn + krp) * lax.rsqrt(ss * (1.0 / n_real) + EPS) * gk)
    k_ref[...] = jnp.concatenate(k_heads, axis=-1).astype(k_ref.dtype)
    if sample:
        lane = lax.broadcasted_iota(jnp.int32, (1, LANE), 1)
        gk_nope = jnp.where(lane < QK_NOPE, gk, 0.0)
        qa = [_dot((q_heads[i] * gk_nope).astype(BF16), wukt_ref[i]) for i in range(M_HEADS)]
        qa_ref[...] = jnp.concatenate(qa, axis=-1)
    else:
        vv_ref[...] = kv[:, hw:hw + vw].astype(vv_ref.dtype)


def _mixer_in(x, mods, lw, tabs, *, sample, t_len, tm, rope):
    n, d = x.shape
    (g_norm_mix, w_in_p, g_v, g_cq, w_uq_p, g_ckv, w_kv_p, gq_p, gk_p, g_mix_g, extra) = lw
    cq_tab, sq_tab = tabs
    gw = g_mix_g.shape[-1]
    qrank = g_cq.shape[-1]
    kvrank = g_ckv.shape[-1]
    gheads = gw // g_v.shape[-1]
    assert w_in_p.shape[1] == 2 * gw + qrank + kvrank + 2 * LANE
    vw = w_kv_p.shape[1] - M_HEADS * LANE
    nt = n // tm
    full = lambda a: pl.BlockSpec(a.shape, lambda i: (0,) * a.ndim)
    if sample:
        mod_spec = lambda k: pl.BlockSpec((tm, d), lambda i, k=k: (i, k))
        tab_spec = pl.BlockSpec((1, LANE), lambda i: (0, 0))
    else:
        tpb = t_len // tm
        mod_spec = lambda k: pl.BlockSpec((None, 1, d), lambda i, k=k: (i // tpb, 0, k))
        tab_spec = pl.BlockSpec((tm, LANE), lambda i: (i % tpb, 0))
    row = lambda w: pl.BlockSpec((tm, w), lambda i: (i, 0))
    in_specs = [row(d), mod_spec(0), mod_spec(1), full(g_norm_mix), full(w_in_p), full(g_v), full(g_cq),
                full(w_uq_p), full(g_ckv), full(w_kv_p), full(gq_p), full(gk_p), tab_spec, tab_spec,
                full(g_mix_g)] + [full(a) for a in extra]
    kr_w = rope
    hw = M_HEADS * LANE
    if sample:
        out_shape = [jax.ShapeDtypeStruct((n, gw), BF16), jax.ShapeDtypeStruct((n, gw), F32),
                     jax.ShapeDtypeStruct((n, hw), F32), jax.ShapeDtypeStruct((n, kvrank), F32),
                     jax.ShapeDtypeStruct((n, kr_w), F32), jax.ShapeDtypeStruct((n, hw), F32),
                     jax.ShapeDtypeStruct((n, M_HEADS * kvrank), F32)]
        out_specs = [row(gw), row(gw), row(hw), row(kvrank), row(kr_w), row(hw), row(M_HEADS * kvrank)]
    else:
        out_shape = [jax.ShapeDtypeStruct((n, gw), BF16), jax.ShapeDtypeStruct((n, gw), F32),
                     jax.ShapeDtypeStruct((n, hw), BF16), jax.ShapeDtypeStruct((n, kvrank), F32),
                     jax.ShapeDtypeStruct((n, kr_w), F32), jax.ShapeDtypeStruct((n, hw), BF16),
                     jax.ShapeDtypeStruct((n, vw), BF16)]
        out_specs = [row(gw), row(gw), row(hw), row(kvrank), row(kr_w), row(hw), row(vw)]
    dims = (gw, qrank, kvrank, gheads, vw)
    return pl.pallas_call(
        functools.partial(_mixer_in_kernel, dims, sample),
        out_shape=out_shape,
        grid=(nt,),
        in_specs=in_specs,
        out_specs=out_specs,
        compiler_params=_cparams(("arbitrary",)),
        name="mixer_in_sample" if sample else "mixer_in_prompt",
    )(x, mods, mods, g_norm_mix, w_in_p, g_v, g_cq, w_uq_p, g_ckv, w_kv_p, gq_p, gk_p, cq_tab, sq_tab,
      g_mix_g, *extra)


def _prompt_attn_kernel(tq, vhead, q_ref, k_ref, v_ref, o_ref):
    qi = pl.program_id(2)
    heads = q_ref.shape[-1] // LANE
    qs = [q_ref[:, h * LANE:(h + 1) * LANE] for h in range(heads)]

    def scores(j, h):
        start = pl.multiple_of(j * tq, tq)
        k = k_ref[pl.ds(start, tq), h * LANE:(h + 1) * LANE]
        v = v_ref[pl.ds(start, tq), h * vhead:(h + 1) * vhead]
        return _dot_nt(qs[h], k), v

    def update(carry, s, v):
        m, l, acc = carry
        m_new = jnp.maximum(m, jnp.max(s, axis=-1, keepdims=True))
        alpha = jnp.exp(m - m_new)
        p = jnp.exp(s - m_new)
        l = l * alpha + jnp.sum(p, axis=-1, keepdims=True)
        acc = acc * alpha + _dot(p.astype(BF16), v)
        return m_new, l, acc

    def body(j, carry):
        return tuple(update(carry[h], *scores(j, h)) for h in range(heads))

    init = tuple((jnp.full((tq, 1), -jnp.inf, F32), jnp.zeros((tq, 1), F32), jnp.zeros((tq, vhead), F32))
                 for _ in range(heads))
    carry = lax.fori_loop(0, qi, body, init)
    causal = (lax.broadcasted_iota(jnp.int32, (tq, tq), 1) <= lax.broadcasted_iota(jnp.int32, (tq, tq), 0))
    outs = []
    for h in range(heads):
        s, v = scores(qi, h)
        m, l, acc = update(carry[h], jnp.where(causal, s, -jnp.inf), v)
        outs.append(acc / l)
    o_ref[...] = jnp.concatenate(outs, axis=-1).astype(o_ref.dtype)


def _prompt_attn(q, k, v, *, batch, t_len, tq):
    n = q.shape[0]
    vhead = v.shape[1] // M_HEADS
    hp = LANE // vhead
    nq = t_len // tq
    return pl.pallas_call(
        functools.partial(_prompt_attn_kernel, tq, vhead),
        out_shape=jax.ShapeDtypeStruct((n, v.shape[1]), F32),
        grid=(batch, M_HEADS // hp, nq),
        in_specs=[pl.BlockSpec((tq, hp * LANE), lambda b, h, i: (b * nq + i, h)),
                  pl.BlockSpec((t_len, hp * LANE), lambda b, h, i: (b, h)),
                  pl.BlockSpec((t_len, hp * vhead), lambda b, h, i: (b, h))],
        out_specs=pl.BlockSpec((tq, hp * vhead), lambda b, h, i: (b * nq + i, h)),
        compiler_params=_cparams(("arbitrary", "arbitrary", "arbitrary")),
        name="prompt_attention",
    )(q, k, v)


def _sample_attn_kernel(layer, n_pages, ppt, tpi, pt_ref, q_ref, qa_ref, knew_ref, cnew_ref, gk_ref, wukt_ref,
                        wuv_ref, ckv_hbm, krt_hbm, o_ref, cbuf, rbuf, sem_c, sem_r, wext_ref, s_scr, p_scr,
                        kt_a, kt_b):
    seq = pl.program_id(0)
    n_seq = pl.num_programs(0)
    slot = lax.rem(seq, 2)
    nxt_slot = 1 - slot
    nxt = jnp.minimum(seq + 1, n_seq - 1)
    rope, page = rbuf.shape[2], rbuf.shape[3]
    tile = ppt * page
    n_tiles = n_pages // ppt
    hn = wukt_ref.shape[0]
    n_real = QK_NOPE + rope

    def page_copies(sq, sl, i):
        pg = pt_ref[sq * n_pages + i]
        return (pltpu.make_async_copy(ckv_hbm.at[layer, pg], cbuf.at[sl, pl.ds(i * page, page)], sem_c.at[sl]),
                pltpu.make_async_copy(krt_hbm.at[layer, pg], rbuf.at[sl, i], sem_r.at[sl]))

    def start_page(sq, sl, i):
        for cp in page_copies(sq, sl, i):
            cp.start()

    def wait_pages(sq, sl):
        def body(i, carry):
            for cp in page_copies(sq, sl, i):
                cp.wait()
            return carry
        lax.fori_loop(0, n_pages, body, 0)

    @pl.when(seq == 0)
    def _():
        def body(i, carry):
            start_page(0, 0, i)
            return carry
        lax.fori_loop(0, n_pages, body, 0)

    wait_pages(seq, slot)

    q = q_ref[0]
    pad = wext_ref.shape[0] - hn - M_HEADS
    wext_ref[...] = jnp.concatenate(
        [wukt_ref[...], qa_ref[0].astype(BF16), jnp.zeros((pad, wext_ref.shape[1]), BF16)], axis=0)
    qr = (q * gk_ref[...])[:, QK_NOPE:QK_NOPE + rope].astype(BF16)

    def c_tile(t):
        return cbuf[slot, pl.ds(pl.multiple_of(t * tile, tile), tile), :].astype(BF16)

    def expand(g, kt_ref):
        for u in range(tpi):
            kt_ref[u] = _dot_nt(wext_ref[...], c_tile(g * tpi + u))

    def score(g, kt_ref):
        for u in range(tpi):
            t = g * tpi + u
            for k in range(ppt):
                start_page(nxt, nxt_slot, t * ppt + k)
            krt = jnp.concatenate([rbuf[slot, t * ppt + k] for k in range(ppt)], axis=1)
            kn = kt_ref[u, :hn, :]
            ss = jnp.sum((kn * kn).reshape(M_HEADS, QK_NOPE, tile), axis=1)
            kr_ss = jnp.sum(krt * krt, axis=0, keepdims=True)
            rinv = lax.rsqrt((ss + kr_ss) * (1.0 / n_real) + EPS)
            s_scr[t] = (kt_ref[u, hn:hn + M_HEADS, :] + _dot(qr, krt.astype(BF16))) * rinv

    n_groups = n_tiles // tpi
    n_pairs = (n_groups - 1) // 2
    expand(0, kt_a)

    def pair(h, carry):
        g = 2 * h
        expand(g + 1, kt_b)
        score(g, kt_a)
        expand(g + 2, kt_a)
        score(g + 1, kt_b)
        return carry

    lax.fori_loop(0, n_pairs, pair, 0)
    g_tail = 2 * n_pairs
    if n_groups - g_tail == 2:
        expand(g_tail + 1, kt_b)
        score(g_tail, kt_a)
        score(g_tail + 1, kt_b)
    else:
        score(g_tail, kt_a)

    s_all = s_scr[...]
    s_new = jnp.sum(q * knew_ref[0], axis=-1, keepdims=True)
    m = jnp.maximum(jnp.max(jnp.max(s_all, axis=0), axis=-1, keepdims=True), s_new)
    p_all = jnp.exp(s_all - m)
    p_new = jnp.exp(s_new - m)
    l = jnp.sum(jnp.sum(p_all, axis=0), axis=-1, keepdims=True) + p_new
    p_scr[...] = p_all

    def phase_c(g, accs):
        return tuple(accs[u] + _dot(p_scr[g * tpi + u].astype(BF16), c_tile(g * tpi + u)) for u in range(tpi))

    zero = jnp.zeros((M_HEADS, cbuf.shape[-1]), F32)
    accs = lax.fori_loop(0, n_tiles // tpi, phase_c, (zero,) * tpi)
    acc = functools.reduce(lambda a, b: a + b, accs)
    c_new = cnew_ref[0].astype(BF16).astype(F32)
    o_lat = (acc + p_new.astype(BF16).astype(F32) * c_new) / l
    full = _dot(o_lat.astype(BF16), wuv_ref[...])
    vhead = full.shape[-1] // M_HEADS
    hrow = lax.broadcasted_iota(jnp.int32, full.shape, 0)
    hcol = lax.broadcasted_iota(jnp.int32, full.shape, 1) // vhead
    o_ref[0] = jnp.sum(jnp.where(hrow == hcol, full, 0.0), axis=0, keepdims=True)

    @pl.when(seq == n_seq - 1)
    def _():
        wait_pages(nxt, nxt_slot)


def _sample_attn(page_table, q, qa, k_new, c_new, gk_p, wukt, wuv, cache_kv, cache_krt, *, layer):
    s_n = q.shape[0]
    n_pages = page_table.shape[1]
    page = cache_kv.shape[2]
    kvrank = cache_kv.shape[3]
    rope = cache_krt.shape[2]
    ppt = max(1, MXU_TILE // page)
    assert n_pages % ppt == 0
    n_tiles = n_pages // ppt
    tpi = 4
    while n_tiles % tpi:
        tpi //= 2
    tile = ppt * page
    vw = wuv.shape[1]
    seq3 = lambda a, b: pl.BlockSpec((1, a, b), lambda s, pt: (s, 0, 0))
    full = lambda a: pl.BlockSpec(a.shape, lambda s, pt: (0,) * a.ndim)
    hbm = pl.BlockSpec(memory_space=pl.ANY)
    in_specs = [seq3(M_HEADS, LANE), seq3(M_HEADS, kvrank), seq3(M_HEADS, LANE), seq3(1, kvrank),
                full(gk_p), full(wukt), full(wuv), hbm, hbm]
    wext_rows = wukt.shape[0] + 2 * M_HEADS
    return pl.pallas_call(
        functools.partial(_sample_attn_kernel, layer, n_pages, ppt, tpi),
        out_shape=jax.ShapeDtypeStruct((s_n, 1, vw), F32),
        grid_spec=pltpu.PrefetchScalarGridSpec(
            num_scalar_prefetch=1, grid=(s_n,), in_specs=in_specs,
            out_specs=pl.BlockSpec((1, 1, vw), lambda s, pt: (s, 0, 0)),
            scratch_shapes=[pltpu.VMEM((2, n_pages * page, kvrank), F32),
                            pltpu.VMEM((2, n_pages, rope, page), F32),
                            pltpu.SemaphoreType.DMA((2,)), pltpu.SemaphoreType.DMA((2,)),
                            pltpu.VMEM((wext_rows, kvrank), BF16),
                            pltpu.VMEM((n_tiles, M_HEADS, tile), F32),
                            pltpu.VMEM((n_tiles, M_HEADS, tile), F32),
                            pltpu.VMEM((tpi, wext_rows, tile), F32),
                            pltpu.VMEM((tpi, wext_rows, tile), F32)]),
        compiler_params=_cparams(("arbitrary",)),
        name="sample_attention",
    )(page_table.reshape(-1), q.reshape(s_n, M_HEADS, LANE), qa.reshape(s_n, M_HEADS, kvrank),
      k_new.reshape(s_n, M_HEADS, LANE), c_new.reshape(s_n, 1, kvrank), gk_p, wukt, wuv,
      cache_kv, cache_krt).reshape(s_n, vw)


def _mixer_out_kernel(n_exp, gn_ref, a_ref, x_ref, gt_ref, sh_ref, sc_ref, gmix_ref, wout_ref, gffn_ref,
                      wr_ref, br_ref, x1_ref, h2_ref, comb_ref):
    gw = gn_ref.shape[-1]
    an = _rms(a_ref[...].astype(F32), gmix_ref[...])
    y = _dot(gn_ref[...], wout_ref[:gw, :]) + _dot(an.astype(BF16), wout_ref[gw:, :])
    x1 = x_ref[...] + gt_ref[...] * y
    x1_ref[...] = x1
    h2 = _rms(x1, gffn_ref[...]) * (1.0 + sc_ref[...]) + sh_ref[...]
    h2_ref[...] = h2.astype(h2_ref.dtype)
    tm = h2.shape[0]

    logits = _dot_nt(wr_ref[...], h2, precision=lax.Precision.HIGHEST)
    scores = jax.nn.sigmoid(logits)
    sel = scores + br_ref[...]
    per = n_exp // N_GROUPS
    best = None
    for g in range(N_GROUPS):
        a, b, c, d = [sel[g * per + i:g * per + i + 1, :] for i in range(per)]
        hi1, lo1 = jnp.maximum(a, b), jnp.minimum(a, b)
        hi2, lo2 = jnp.maximum(c, d), jnp.minimum(c, d)
        gs = jnp.maximum(hi1, hi2) + jnp.maximum(jnp.minimum(hi1, hi2), jnp.maximum(lo1, lo2))
        if best is None:
            best, grp = gs, jnp.zeros(gs.shape, jnp.int32)
        else:
            better = gs > best
            grp = jnp.where(better, g, grp)
            best = jnp.where(better, gs, best)
    erow = lax.broadcasted_iota(jnp.int32, (n_exp, tm), 0)
    selm = jnp.where(erow // per == grp, sel, -jnp.inf)
    m1 = jnp.max(selm, axis=0, keepdims=True)
    i1 = jnp.min(jnp.where(selm == m1, erow, n_exp), axis=0, keepdims=True)
    oh1 = erow == i1
    selm2 = jnp.where(oh1, -jnp.inf, selm)
    m2 = jnp.max(selm2, axis=0, keepdims=True)
    i2 = jnp.min(jnp.where(selm2 == m2, erow, n_exp), axis=0, keepdims=True)
    oh2 = erow == i2
    s1 = jnp.sum(jnp.where(oh1, scores, 0.0), axis=0, keepdims=True)
    s2 = jnp.sum(jnp.where(oh2, scores, 0.0), axis=0, keepdims=True)
    tot = s1 + s2
    comb_t = jnp.where(oh1, s1 / tot, 0.0) + jnp.where(oh2, s2 / tot, 0.0)
    comb_pad = jnp.concatenate([comb_t, jnp.zeros((LANE - n_exp, tm), F32)], axis=0)
    comb_ref[...] = comb_pad.T


def _mixer_out(gn, attn, x, mods, g_mix_a, w_out_b, g_norm_ffn, w_router_t, b_router, *, sample, t_len, tm):
    n, d = x.shape
    gw = gn.shape[1]
    n_exp = w_router_t.shape[0]
    full = lambda a: pl.BlockSpec(a.shape, lambda i: (0,) * a.ndim)
    row = lambda w: pl.BlockSpec((tm, w), lambda i: (i, 0))
    if sample:
        mod_spec = lambda k: pl.BlockSpec((tm, d), lambda i, k=k: (i, k))
    else:
        tpb = t_len // tm
        mod_spec = lambda k: pl.BlockSpec((None, 1, d), lambda i, k=k: (i // tpb, 0, k))
    return pl.pallas_call(
        functools.partial(_mixer_out_kernel, n_exp),
        out_shape=[jax.ShapeDtypeStruct((n, d), F32), jax.ShapeDtypeStruct((n, d), BF16),
                   jax.ShapeDtypeStruct((n, LANE), F32)],
        grid=(n // tm,),
        in_specs=[row(gw), row(attn.shape[1]), row(d), mod_spec(2), mod_spec(3), mod_spec(4), full(g_mix_a),
                  full(w_out_b), full(g_norm_ffn), full(w_router_t), full(b_router)],
        out_specs=[row(d), row(d), row(LANE)],
        compiler_params=_cparams(("arbitrary",)),
        name="mixer_out_sample" if sample else "mixer_out_prompt",
    )(gn, attn, x, mods, mods, mods, g_mix_a, w_out_b, g_norm_ffn, w_router_t, b_router)


def _moe_kernel(h_ref, comb_ref, x1_ref, gt_ref, wg_ref, wu_ref, wd_ref, o_ref, acc_ref):
    e = pl.program_id(1)

    @pl.when(e == 0)
    def _():
        acc_ref[...] = jnp.zeros(acc_ref.shape, F32)

    h = h_ref[...]
    comb = comb_ref[...]
    lane = lax.broadcasted_iota(jnp.int32, comb.shape, 1)
    ce = jnp.sum(jnp.where(lane == e, comb, 0.0), axis=-1, keepdims=True)
    a = _dot(h, wg_ref[...].astype(BF16))
    b = _dot(h, wu_ref[...].astype(BF16))
    act = (a * jax.nn.sigmoid(a)) * b * ce
    acc_ref[...] += _dot(act.astype(BF16), wd_ref[...].astype(BF16))

    @pl.when(e == pl.num_programs(1) - 1)
    def _():
        o_ref[...] = x1_ref[...] + gt_ref[...] * acc_ref[...]


def _moe(h2, comb, x1, mods, w_gate, w_up, w_down, *, layer, sample, t_len, tm):
    n, d = x1.shape
    n_exp, _, f = w_gate.shape[1:]
    row = lambda w: pl.BlockSpec((tm, w), lambda i, e: (i, 0))
    if sample:
        mod_spec = pl.BlockSpec((tm, d), lambda i, e: (i, 5))
    else:
        tpb = t_len // tm
        mod_spec = pl.BlockSpec((None, 1, d), lambda i, e: (i // tpb, 0, 5))
    return pl.pallas_call(
        _moe_kernel,
        out_shape=jax.ShapeDtypeStruct((n, d), F32),
        grid=(n // tm, n_exp),
        in_specs=[row(d), row(LANE), row(d), mod_spec,
                  pl.BlockSpec((None, None, d, f), lambda i, e: (layer, e, 0, 0)),
                  pl.BlockSpec((None, None, d, f), lambda i, e: (layer, e, 0, 0)),
                  pl.BlockSpec((None, None, f, d), lambda i, e: (layer, e, 0, 0))],
        out_specs=row(d),
        scratch_shapes=[pltpu.VMEM((tm, d), F32)],
        compiler_params=_cparams(("arbitrary", "arbitrary")),
        name="moe_sample" if sample else "moe_prompt",
    )(h2, comb, x1, mods, w_gate, w_up, w_down)


def _rope_tables(pos, rope):
    half = rope // 2
    freqs = ROPE_THETA ** (-jnp.arange(half, dtype=F32) / half)
    ang = pos.astype(F32)[:, None] * freqs[None, :]
    cos, sin = jnp.cos(ang), jnp.sin(ang)
    t = pos.shape[0]
    pad = LANE - QK_NOPE - rope
    cq = jnp.concatenate([jnp.ones((t, QK_NOPE), F32), cos, cos, jnp.zeros((t, pad), F32)], axis=-1)
    sq = jnp.concatenate([jnp.zeros((t, QK_NOPE), F32), sin, sin, jnp.zeros((t, pad), F32)], axis=-1)
    return cq, sq


def _rot_cols(w, rope):
    half = rope // 2
    return jnp.concatenate([-w[..., half:], w[..., :half]], axis=-1)


def _prep_layer(l, w_in, w_uq, w_ukv, g_qk_q, g_qk_k, rope, gw, qrank, kvrank, vhead):
    d = w_in.shape[1]
    qk_head = QK_NOPE + rope
    pad = LANE - qk_head
    o = 2 * gw + qrank + kvrank
    w_kr = w_in[l][:, o:o + rope]
    z64 = jnp.zeros((d, QK_NOPE), F32)
    zp = jnp.zeros((d, pad), F32)
    w_in_p = jnp.concatenate([w_in[l][:, :o], z64, w_kr, zp, z64, _rot_cols(w_kr, rope), zp], axis=-1).astype(BF16)
    wq = w_uq[l].reshape(qrank, M_HEADS, qk_head)
    zq = jnp.zeros((qrank, M_HEADS, pad), F32)
    wq_plain = jnp.concatenate([wq, zq], axis=-1).reshape(qrank, M_HEADS * LANE)
    wq_rot = jnp.concatenate([jnp.zeros((qrank, M_HEADS, QK_NOPE), F32), _rot_cols(wq[..., QK_NOPE:], rope), zq],
                             axis=-1).reshape(qrank, M_HEADS * LANE)
    w_uq_p = jnp.concatenate([wq_plain, wq_rot], axis=-1).astype(BF16)
    wkv = w_ukv[l].reshape(kvrank, M_HEADS, QK_NOPE + vhead)
    wk = wkv[..., :QK_NOPE]
    wk_pad = jnp.concatenate([wk, jnp.zeros((kvrank, M_HEADS, LANE - QK_NOPE), F32)], axis=-1)
    wv = wkv[..., QK_NOPE:].reshape(kvrank, M_HEADS * vhead)
    w_kv_p = jnp.concatenate([wk_pad.reshape(kvrank, M_HEADS * LANE), wv], axis=-1).astype(BF16)
    wukt_pad = jnp.concatenate([wk.transpose(1, 2, 0), jnp.zeros((M_HEADS, LANE - QK_NOPE, kvrank), F32)],
                               axis=1).astype(BF16)
    wukt = wk.transpose(1, 2, 0).reshape(M_HEADS * QK_NOPE, kvrank).astype(BF16)
    zg = jnp.zeros((pad,), F32)
    gq_p = (jnp.concatenate([g_qk_q[l], zg]) * (qk_head ** -0.5)).reshape(1, LANE)
    gk_p = jnp.concatenate([g_qk_k[l], zg]).reshape(1, LANE)
    return w_in_p, w_uq_p, w_kv_p, wukt_pad, wukt, wv.astype(BF16), gq_p, gk_p


def kernel(x_prompt, x_sample, c_prompt, c_sample, cache_kv_latent, cache_k_rope, page_table, w_ada, b_ada,
           g_norm_mix, g_norm_ffn, w_in, g_v, w_s, b_s, g_cq, w_uq, g_ckv, w_ukv, g_qk_q, g_qk_k, g_mix_out,
           w_out, w_router, b_router, w_gate, w_up, w_down):
    batch, t_p, d = x_prompt.shape
    s_n, t_s, _ = x_sample.shape
    depth = w_ada.shape[0]
    gheads, chunk = w_s.shape[1], w_s.shape[2]
    hd = g_v.shape[-1]
    gw = gheads * hd
    qrank = g_cq.shape[-1]
    kvrank = g_ckv.shape[-1]
    rope = cache_k_rope.shape[-1]
    vhead = w_ukv.shape[-1] // M_HEADS - QK_NOPE
    past = page_table.shape[1] * cache_kv_latent.shape[2]
    assert t_s == 1 and t_p % chunk == 0 and g_qk_q.shape[-1] == QK_NOPE + rope
    row2 = lambda a: a.reshape(1, -1)

    mods = _ada(jnp.concatenate([c_prompt, c_sample], axis=0), w_ada, b_ada)
    tab_p = _rope_tables(jnp.arange(t_p, dtype=jnp.int32), rope)
    tab_s = _rope_tables(past + jnp.arange(t_s, dtype=jnp.int32), rope)
    w_router_t = w_router.T
    b_router_c = b_router.reshape(-1, 1)

    tm_p = 256
    tm_moe = 1024 if t_p % 1024 == 0 else tm_p
    tm_s = s_n
    tq = 512 if t_p % 512 == 0 else 256
    cache_krt = jnp.swapaxes(cache_k_rope, 2, 3)
    xp = x_prompt.reshape(batch * t_p, d)
    xs = x_sample.reshape(s_n, d)
    open_p = ((t_p - 1) // chunk) * chunk
    outs = [[] for _ in range(6)]
    for l in range(depth):
        w_in_p, w_uq_p, w_kv_p, wukt_pad, wukt, wuv, gq_p, gk_p = _prep_layer(
            l, w_in, w_uq, w_ukv, g_qk_q, g_qk_k, rope, gw, qrank, kvrank, vhead)
        g_mix_g, g_mix_a = row2(g_mix_out[l, :gw]), row2(g_mix_out[l, gw:])
        w_out_b = w_out[l].astype(BF16)
        base = (row2(g_norm_mix[l]), w_in_p, row2(g_v[l]), row2(g_cq[l]), w_uq_p, row2(g_ckv[l]), w_kv_p,
                gq_p, gk_p, g_mix_g)
        mp = mods[l, :batch].reshape(batch, 1, N_MOD * d)
        ms = mods[l, batch:]

        bs_full = jnp.broadcast_to(b_s[l][:, :, None], (gheads, chunk, hd))
        gn, v, q, ckv, kr, k, vv = _mixer_in(xp, mp, base + ((w_s[l], bs_full),), tab_p,
                                             sample=False, t_len=t_p, tm=tm_p, rope=rope)
        attn = _prompt_attn(q, k, vv, batch=batch, t_len=t_p, tq=tq)
        x1, h2, comb = _mixer_out(gn, attn, xp, mp, g_mix_a, w_out_b, row2(g_norm_ffn[l]), w_router_t,
                                  b_router_c, sample=False, t_len=t_p, tm=tm_p)
        xp = _moe(h2, comb, x1, mp, w_gate, w_up, w_down, layer=l, sample=False, t_len=t_p, tm=tm_moe)
        outs[0].append(ckv.reshape(batch, t_p, kvrank))
        outs[1].append(kr.reshape(batch, t_p, rope))
        outs[2].append(v.reshape(batch, t_p, gheads, hd)[:, open_p:])

        w00 = row2(jnp.repeat(w_s[l, :, 0, 0], hd))
        b0 = row2(jnp.repeat(b_s[l, :, 0], hd))
        gn, v, q, ckv, kr, k, qa = _mixer_in(xs, ms, base + ((w00, b0, wukt_pad),), tab_s,
                                             sample=True, t_len=t_s, tm=tm_s, rope=rope)
        attn = _sample_attn(page_table, q, qa, k, ckv, gk_p, wukt, wuv, cache_kv_latent, cache_krt, layer=l)
        x1, h2, comb = _mixer_out(gn, attn, xs, ms, g_mix_a, w_out_b, row2(g_norm_ffn[l]), w_router_t,
                                  b_router_c, sample=True, t_len=t_s, tm=tm_s)
        xs = _moe(h2, comb, x1, ms, w_gate, w_up, w_down, layer=l, sample=True, t_len=t_s, tm=tm_s)
        outs[3].append(ckv.reshape(s_n, t_s, kvrank))
        outs[4].append(kr.reshape(s_n, t_s, rope))
        outs[5].append(v.reshape(s_n, t_s, gheads, hd))

    return (xp.reshape(batch, t_p, d), xs.reshape(s_n, t_s, d), jnp.stack(outs[0]), jnp.stack(outs[1]),
            jnp.stack(outs[2]), jnp.stack(outs[3]), jnp.stack(outs[4]), jnp.stack(outs[5]))
```

```python
import functools

import jax
import jax.numpy as jnp
from jax import lax
from jax.experimental import pallas as pl
from jax.experimental.pallas import tpu as pltpu

F32 = jnp.float32
BF16 = jnp.bfloat16

M_HEADS = 8
QK_NOPE = 64
N_GROUPS = 4
N_MOD = 6
ROPE_THETA = 10000.0
EPS = 1e-6
LANE = 128
MXU_TILE = 256
VMEM_LIMIT = 56 * 1024 * 1024


def _cparams(sem):
    return pltpu.CompilerParams(dimension_semantics=sem, vmem_limit_bytes=VMEM_LIMIT)


def _dot(a, b):
    return jnp.dot(a, b, preferred_element_type=F32)


def _dot_nt(a, b, precision=None):
    return lax.dot_general(a, b, (((1,), (1,)), ((), ())), preferred_element_type=F32,
                           precision=precision)


def _rms(x, g, n=None):
    n = x.shape[-1] if n is None else n
    ms = jnp.sum(x * x, axis=-1, keepdims=True) * (1.0 / n)
    return x * lax.rsqrt(ms + EPS) * g


def _ada_kernel(c_ref, w_ref, b_ref, o_ref):
    c = c_ref[...]
    s = c * jax.nn.sigmoid(c)
    o_ref[0] = _dot(s.astype(BF16), w_ref[0].astype(BF16)) + b_ref[0]


def _ada(c_all, w_ada, b_ada):
    depth, d, n6 = w_ada.shape
    rows = c_all.shape[0]
    tn = 1536
    return pl.pallas_call(
        _ada_kernel,
        out_shape=jax.ShapeDtypeStruct((depth, rows, n6), F32),
        grid=(depth, n6 // tn),
        in_specs=[pl.BlockSpec((rows, d), lambda l, j: (0, 0)),
                  pl.BlockSpec((1, d, tn), lambda l, j: (l, 0, j)),
                  pl.BlockSpec((1, 1, tn), lambda l, j: (l, 0, j))],
        out_specs=pl.BlockSpec((1, rows, tn), lambda l, j: (l, 0, j)),
        compiler_params=_cparams(("arbitrary", "arbitrary")),
        name="ada_modulation",
    )(c_all, w_ada, b_ada.reshape(depth, 1, n6))


def _mixer_in_kernel(dims, sample, *refs):
    gw, qrank, kvrank, gheads, vw = dims
    (x_ref, sh_ref, sc_ref, gnm_ref, win_ref, gv_ref, gcq_ref, wuq_ref, gckv_ref, wkv_ref,
     gq_ref, gk_ref, cq_ref, sq_ref, gmix_ref) = refs[:15]
    if sample:
        w00_ref, b0_ref, wukt_ref = refs[15:18]
        gn_ref, v_ref, q_ref, ckv_ref, kr_ref, k_ref, qa_ref = refs[18:]
    else:
        ws_ref, bs_ref = refs[15:17]
        gn_ref, v_ref, q_ref, ckv_ref, kr_ref, k_ref, vv_ref = refs[17:]

    x = x_ref[...]
    h = _rms(x, gnm_ref[...]) * (1.0 + sc_ref[...]) + sh_ref[...]
    z = _dot(h.astype(BF16), win_ref[...])
    tm = z.shape[0]
    u = jax.nn.gelu(z[:, :gw])
    vg = jax.nn.gelu(z[:, gw:2 * gw])
    hd = gw // gheads
    gv = gv_ref[...]
    v = jnp.concatenate([_rms(vg[:, i * hd:(i + 1) * hd], gv) for i in range(gheads)], axis=-1)
    v_ref[...] = v

    if sample:
        s = v * w00_ref[...] + b0_ref[...]
    else:
        chunk = ws_ref.shape[-1]
        row = lax.broadcasted_iota(jnp.int32, (chunk, chunk), 0)
        col = lax.broadcasted_iota(jnp.int32, (chunk, chunk), 1)
        vb = v.astype(BF16)
        cols = []
        for i in range(gheads):
            wt = jnp.where(col <= row, ws_ref[i], 0.0).astype(BF16)
            rows = [_dot(wt, vb[c * chunk:(c + 1) * chunk, i * hd:(i + 1) * hd]) + bs_ref[i]
                    for c in range(tm // chunk)]
            cols.append(jnp.concatenate(rows, axis=0) if len(rows) > 1 else rows[0])
        s = jnp.concatenate(cols, axis=-1)
    g = u * s
    gn_ref[...] = _rms(g, gmix_ref[...]).astype(gn_ref.dtype)

    o0 = 2 * gw
    cq = _rms(z[:, o0:o0 + qrank], gcq_ref[...])
    qq = _dot(cq.astype(BF16), wuq_ref[...])
    hw = M_HEADS * LANE
    cqt = cq_ref[...]
    sqt = sq_ref[...]
    gq = gq_ref[...]
    n_real = QK_NOPE + kr_ref.shape[-1]
    q_heads = []
    for i in range(M_HEADS):
        qh = qq[:, i * LANE:(i + 1) * LANE] * cqt + qq[:, hw + i * LANE:hw + (i + 1) * LANE] * sqt
        q_heads.append(_rms(qh, gq, n_real))
    q = jnp.concatenate(q_heads, axis=-1)
    q_ref[...] = q.astype(q_ref.dtype)

    o1 = o0 + qrank
    ckv = _rms(z[:, o1:o1 + kvrank], gckv_ref[...])
    ckv_ref[...] = ckv
    o2 = o1 + kvrank
    krp = z[:, o2:o2 + LANE] * cqt + z[:, o2 + LANE:o2 + 2 * LANE] * sqt
    rope = kr_ref.shape[-1]
    kr_ref[...] = krp[:, QK_NOPE:QK_NOPE + rope]
    kv = _dot(ckv.astype(BF16), wkv_ref[...])
    kr_ss = jnp.sum(krp * krp, axis=-1, keepdims=True)
    gk = gk_ref[...]
    k_heads = []
    for i in range(M_HEADS):
        kn = kv[:, i * LANE:(i + 1) * LANE]
        ss = jnp.sum(kn * kn, axis=-1, keepdims=True) + kr_ss
        k_heads.append((kn + krp) * lax.rsqrt(ss * (1.0 / n_real) + EPS) * gk)
    k_ref[...] = jnp.concatenate(k_heads, axis=-1).astype(k_ref.dtype)
    if sample:
        lane = lax.broadcasted_iota(jnp.int32, (1, LANE), 1)
        gk_nope = jnp.where(lane < QK_NOPE, gk, 0.0)
        qa = [_dot((q_heads[i] * gk_nope).astype(BF16), wukt_ref[i]) for i in range(M_HEADS)]
        qa_ref[...] = jnp.concatenate(qa, axis=-1)
    else:
        vv_ref[...] = kv[:, hw:hw + vw].astype(vv_ref.dtype)


def _mixer_in(x, mods, lw, tabs, *, sample, t_len, tm, rope):
    n, d = x.shape
    (g_norm_mix, w_in_p, g_v, g_cq, w_uq_p, g_ckv, w_kv_p, gq_p, gk_p, g_mix_g, extra) = lw
    cq_tab, sq_tab = tabs
    gw = g_mix_g.shape[-1]
    qrank = g_cq.shape[-1]
    kvrank = g_ckv.shape[-1]
    gheads = gw // g_v.shape[-1]
    assert w_in_p.shape[1] == 2 * gw + qrank + kvrank + 2 * LANE
    vw = w_kv_p.shape[1] - M_HEADS * LANE
    nt = n // tm
    full = lambda a: pl.BlockSpec(a.shape, lambda i: (0,) * a.ndim)
    if sample:
        mod_spec = lambda k: pl.BlockSpec((tm, d), lambda i, k=k: (i, k))
        tab_spec = pl.BlockSpec((1, LANE), lambda i: (0, 0))
    else:
        tpb = t_len // tm
        mod_spec = lambda k: pl.BlockSpec((None, 1, d), lambda i, k=k: (i // tpb, 0, k))
        tab_spec = pl.BlockSpec((tm, LANE), lambda i: (i % tpb, 0))
    row = lambda w: pl.BlockSpec((tm, w), lambda i: (i, 0))
    in_specs = [row(d), mod_spec(0), mod_spec(1), full(g_norm_mix), full(w_in_p), full(g_v), full(g_cq),
                full(w_uq_p), full(g_ckv), full(w_kv_p), full(gq_p), full(gk_p), tab_spec, tab_spec,
                full(g_mix_g)] + [full(a) for a in extra]
    kr_w = rope
    hw = M_HEADS * LANE
    if sample:
        out_shape = [jax.ShapeDtypeStruct((n, gw), BF16), jax.ShapeDtypeStruct((n, gw), F32),
                     jax.ShapeDtypeStruct((n, hw), F32), jax.ShapeDtypeStruct((n, kvrank), F32),
                     jax.ShapeDtypeStruct((n, kr_w), F32), jax.ShapeDtypeStruct((n, hw), F32),
                     jax.ShapeDtypeStruct((n, M_HEADS * kvrank), F32)]
        out_specs = [row(gw), row(gw), row(hw), row(kvrank), row(kr_w), row(hw), row(M_HEADS * kvrank)]
    else:
        out_shape = [jax.ShapeDtypeStruct((n, gw), BF16), jax.ShapeDtypeStruct((n, gw), F32),
                     jax.ShapeDtypeStruct((n, hw), BF16), jax.ShapeDtypeStruct((n, kvrank), F32),
                     jax.ShapeDtypeStruct((n, kr_w), F32), jax.ShapeDtypeStruct((n, hw), BF16),
                     jax.ShapeDtypeStruct((n, vw), BF16)]
        out_specs = [row(gw), row(gw), row(hw), row(kvrank), row(kr_w), row(hw), row(vw)]
    dims = (gw, qrank, kvrank, gheads, vw)
    return pl.pallas_call(
        functools.partial(_mixer_in_kernel, dims, sample),
        out_shape=out_shape,
        grid=(nt,),
        in_specs=in_specs,
        out_specs=out_specs,
        compiler_params=_cparams(("arbitrary",)),
        name="mixer_in_sample" if sample else "mixer_in_prompt",
    )(x, mods, mods, g_norm_mix, w_in_p, g_v, g_cq, w_uq_p, g_ckv, w_kv_p, gq_p, gk_p, cq_tab, sq_tab,
      g_mix_g, *extra)


def _prompt_attn_kernel(tq, vhead, q_ref, k_ref, v_ref, o_ref):
    qi = pl.program_id(2)
    heads = q_ref.shape[-1] // LANE
    qs = [q_ref[:, h * LANE:(h + 1) * LANE] for h in range(heads)]

    def scores(j, h):
        start = pl.multiple_of(j * tq, tq)
        k = k_ref[pl.ds(start, tq), h * LANE:(h + 1) * LANE]
        v = v_ref[pl.ds(start, tq), h * vhead:(h + 1) * vhead]
        return _dot_nt(qs[h], k), v

    def update(carry, s, v):
        m, l, acc = carry
        m_new = jnp.maximum(m, jnp.max(s, axis=-1, keepdims=True))
        alpha = jnp.exp(m - m_new)
        p = jnp.exp(s - m_new)
        l = l * alpha + jnp.sum(p, axis=-1, keepdims=True)
        acc = acc * alpha + _dot(p.astype(BF16), v)
        return m_new, l, acc

    def body(j, carry):
        return tuple(update(carry[h], *scores(j, h)) for h in range(heads))

    init = tuple((jnp.full((tq, 1), -jnp.inf, F32), jnp.zeros((tq, 1), F32), jnp.zeros((tq, vhead), F32))
                 for _ in range(heads))
    carry = lax.fori_loop(0, qi, body, init)
    causal = (lax.broadcasted_iota(jnp.int32, (tq, tq), 1) <= lax.broadcasted_iota(jnp.int32, (tq, tq), 0))
    outs = []
    for h in range(heads):
        s, v = scores(qi, h)
        m, l, acc = update(carry[h], jnp.where(causal, s, -jnp.inf), v)
        outs.append(acc / l)
    o_ref[...] = jnp.concatenate(outs, axis=-1).astype(o_ref.dtype)


def _prompt_attn(q, k, v, *, batch, t_len, tq):
    n = q.shape[0]
    vhead = v.shape[1] // M_HEADS
    hp = LANE // vhead
    nq = t_len // tq
    return pl.pallas_call(
        functools.partial(_prompt_attn_kernel, tq, vhead),
        out_shape=jax.ShapeDtypeStruct((n, v.shape[1]), F32),
        grid=(batch, M_HEADS // hp, nq),
        in_specs=[pl.BlockSpec((tq, hp * LANE), lambda b, h, i: (b * nq + i, h)),
                  pl.BlockSpec((t_len, hp * LANE), lambda b, h, i: (b, h)),
                  pl.BlockSpec((t_len, hp * vhead), lambda b, h, i: (b, h))],
        out_specs=pl.BlockSpec((tq, hp * vhead), lambda b, h, i: (b * nq + i, h)),
        compiler_params=_cparams(("arbitrary", "arbitrary", "arbitrary")),
        name="prompt_attention",
    )(q, k, v)


def _sample_attn_kernel(layer, n_pages, ppt, tpi, pt_ref, q_ref, qa_ref, knew_ref, cnew_ref, gk_ref, wukt_ref,
                        wuv_ref, ckv_hbm, krt_hbm, o_ref, cbuf, rbuf, sem_c, sem_r, wext_ref, s_scr, p_scr,
                        kt_a, kt_b):
    seq = pl.program_id(0)
    n_seq = pl.num_programs(0)
    slot = lax.rem(seq, 2)
    nxt_slot = 1 - slot
    nxt = jnp.minimum(seq + 1, n_seq - 1)
    rope, page = rbuf.shape[2], rbuf.shape[3]
    tile = ppt * page
    n_tiles = n_pages // ppt
    hn = wukt_ref.shape[0]
    n_real = QK_NOPE + rope

    def page_copies(sq, sl, i):
        pg = pt_ref[sq * n_pages + i]
        return (pltpu.make_async_copy(ckv_hbm.at[layer, pg], cbuf.at[sl, pl.ds(i * page, page)], sem_c.at[sl]),
                pltpu.make_async_copy(krt_hbm.at[layer, pg], rbuf.at[sl, i], sem_r.at[sl]))

    def start_page(sq, sl, i):
        for cp in page_copies(sq, sl, i):
            cp.start()

    def wait_pages(sq, sl):
        def body(i, carry):
            for cp in page_copies(sq, sl, i):
                cp.wait()
            return carry
        lax.fori_loop(0, n_pages, body, 0)

    @pl.when(seq == 0)
    def _():
        def body(i, carry):
            start_page(0, 0, i)
            return carry
        lax.fori_loop(0, n_pages, body, 0)

    wait_pages(seq, slot)

    q = q_ref[0]
    pad = wext_ref.shape[0] - hn - M_HEADS
    wext_ref[...] = jnp.concatenate(
        [wukt_ref[...], qa_ref[0].astype(BF16), jnp.zeros((pad, wext_ref.shape[1]), BF16)], axis=0)
    qr = (q * gk_ref[...])[:, QK_NOPE:QK_NOPE + rope].astype(BF16)

    def c_tile(t):
        return cbuf[slot, pl.ds(pl.multiple_of(t * tile, tile), tile), :].astype(BF16)

    def expand(g, kt_ref):
        for u in range(tpi):
            kt_ref[u] = _dot_nt(wext_ref[...], c_tile(g * tpi + u))

    def score(g, kt_ref):
        for u in range(tpi):
            t = g * tpi + u
            for k in range(ppt):
                start_page(nxt, nxt_slot, t * ppt + k)
            krt = jnp.concatenate([rbuf[slot, t * ppt + k] for k in range(ppt)], axis=1)
            kn = kt_ref[u, :hn, :]
            ss = jnp.sum((kn * kn).reshape(M_HEADS, QK_NOPE, tile), axis=1)
            kr_ss = jnp.sum(krt * krt, axis=0, keepdims=True)
            rinv = lax.rsqrt((ss + kr_ss) * (1.0 / n_real) + EPS)
            s_scr[t] = (kt_ref[u, hn:hn + M_HEADS, :] + _dot(qr, krt.astype(BF16))) * rinv

    n_groups = n_tiles // tpi
    n_pairs = (n_groups - 1) // 2
    expand(0, kt_a)

    def pair(h, carry):
        g = 2 * h
        expand(g + 1, kt_b)
        score(g, kt_a)
        expand(g + 2, kt_a)
        score(g + 1, kt_b)
        return carry

    lax.fori_loop(0, n_pairs, pair, 0)
    g_tail = 2 * n_pairs
    if n_groups - g_tail == 2:
        expand(g_tail + 1, kt_b)
        score(g_tail, kt_a)
        score(g_tail + 1, kt_b)
    else:
        score(g_tail, kt_a)

    s_all = s_scr[...]
    s_new = jnp.sum(q * knew_ref[0], axis=-1, keepdims=True)
    m = jnp.maximum(jnp.max(jnp.max(s_all, axis=0), axis=-1, keepdims=True), s_new)
    p_all = jnp.exp(s_all - m)
    p_new = jnp.exp(s_new - m)
    l = jnp.sum(jnp.sum(p_all, axis=0), axis=-1, keepdims=True) + p_new
    p_scr[...] = p_all

    def phase_c(g, accs):
        return tuple(accs[u] + _dot(p_scr[g * tpi + u].astype(BF16), c_tile(g * tpi + u)) for u in range(tpi))

    zero = jnp.zeros((M_HEADS, cbuf.shape[-1]), F32)
    accs = lax.fori_loop(0, n_tiles // tpi, phase_c, (zero,) * tpi)
    acc = functools.reduce(lambda a, b: a + b, accs)
    c_new = cnew_ref[0].astype(BF16).astype(F32)
    o_lat = (acc + p_new.astype(BF16).astype(F32) * c_new) / l
    full = _dot(o_lat.astype(BF16), wuv_ref[...])
    vhead = full.shape[-1] // M_HEADS
    hrow = lax.broadcasted_iota(jnp.int32, full.shape, 0)
    hcol = lax.broadcasted_iota(jnp.int32, full.shape, 1) // vhead
    o_ref[0] = jnp.sum(jnp.where(hrow == hcol, full, 0.0), axis=0, keepdims=True)

    @pl.when(seq == n_seq - 1)
    def _():
        wait_pages(nxt, nxt_slot)


def _sample_attn(page_table, q, qa, k_new, c_new, gk_p, wukt, wuv, cache_kv, cache_krt, *, layer):
    s_n = q.shape[0]
    n_pages = page_table.shape[1]
    page = cache_kv.shape[2]
    kvrank = cache_kv.shape[3]
    rope = cache_krt.shape[2]
    ppt = max(1, MXU_TILE // page)
    assert n_pages % ppt == 0
    n_tiles = n_pages // ppt
    tpi = 4
    while n_tiles % tpi:
        tpi //= 2
    tile = ppt * page
    vw = wuv.shape[1]
    seq3 = lambda a, b: pl.BlockSpec((1, a, b), lambda s, pt: (s, 0, 0))
    full = lambda a: pl.BlockSpec(a.shape, lambda s, pt: (0,) * a.ndim)
    hbm = pl.BlockSpec(memory_space=pl.ANY)
    in_specs = [seq3(M_HEADS, LANE), seq3(M_HEADS, kvrank), seq3(M_HEADS, LANE), seq3(1, kvrank),
                full(gk_p), full(wukt), full(wuv), hbm, hbm]
    wext_rows = wukt.shape[0] + 2 * M_HEADS
    return pl.pallas_call(
        functools.partial(_sample_attn_kernel, layer, n_pages, ppt, tpi),
        out_shape=jax.ShapeDtypeStruct((s_n, 1, vw), F32),
        grid_spec=pltpu.PrefetchScalarGridSpec(
            num_scalar_prefetch=1, grid=(s_n,), in_specs=in_specs,
            out_specs=pl.BlockSpec((1, 1, vw), lambda s, pt: (s, 0, 0)),
            scratch_shapes=[pltpu.VMEM((2, n_pages * page, kvrank), F32),
                            pltpu.VMEM((2, n_pages, rope, page), F32),
                            pltpu.SemaphoreType.DMA((2,)), pltpu.SemaphoreType.DMA((2,)),
                            pltpu.VMEM((wext_rows, kvrank), BF16),
                            pltpu.VMEM((n_tiles, M_HEADS, tile), F32),
                            pltpu.VMEM((n_tiles, M_HEADS, tile), F32),
                            pltpu.VMEM((tpi, wext_rows, tile), F32),
                            pltpu.VMEM((tpi, wext_rows, tile), F32)]),
        compiler_params=_cparams(("arbitrary",)),
        name="sample_attention",
    )(page_table.reshape(-1), q.reshape(s_n, M_HEADS, LANE), qa.reshape(s_n, M_HEADS, kvrank),
      k_new.reshape(s_n, M_HEADS, LANE), c_new.reshape(s_n, 1, kvrank), gk_p, wukt, wuv,
      cache_kv, cache_krt).reshape(s_n, vw)


def _mixer_out_kernel(n_exp, gn_ref, a_ref, x_ref, gt_ref, sh_ref, sc_ref, gmix_ref, wout_ref, gffn_ref,
                      wr_ref, br_ref, cnt_in_ref, x1_ref, h2_ref, route_ref, wt_ref, cnt_ref, run_ref):
    @pl.when(pl.program_id(0) == 0)
    def _():
        run_ref[...] = cnt_in_ref[...]

    gw = gn_ref.shape[-1]
    an = _rms(a_ref[...].astype(F32), gmix_ref[...])
    y = _dot(gn_ref[...], wout_ref[:gw, :]) + _dot(an.astype(BF16), wout_ref[gw:, :])
    x1 = x_ref[...] + gt_ref[...] * y
    x1_ref[...] = x1
    h2 = _rms(x1, gffn_ref[...]) * (1.0 + sc_ref[...]) + sh_ref[...]
    h2_ref[...] = h2.astype(h2_ref.dtype)
    tm = h2.shape[0]

    logits = _dot_nt(wr_ref[...], h2, precision=lax.Precision.HIGHEST)
    scores = jax.nn.sigmoid(logits)
    sel = scores + br_ref[...]
    per = n_exp // N_GROUPS
    best = None
    for g in range(N_GROUPS):
        a, b, c, d = [sel[g * per + i:g * per + i + 1, :] for i in range(per)]
        hi1, lo1 = jnp.maximum(a, b), jnp.minimum(a, b)
        hi2, lo2 = jnp.maximum(c, d), jnp.minimum(c, d)
        gs = jnp.maximum(hi1, hi2) + jnp.maximum(jnp.minimum(hi1, hi2), jnp.maximum(lo1, lo2))
        if best is None:
            best, grp = gs, jnp.zeros(gs.shape, jnp.int32)
        else:
            better = gs > best
            grp = jnp.where(better, g, grp)
            best = jnp.where(better, gs, best)
    erow = lax.broadcasted_iota(jnp.int32, (n_exp, tm), 0)
    selm = jnp.where(erow // per == grp, sel, -jnp.inf)
    m1 = jnp.max(selm, axis=0, keepdims=True)
    i1 = jnp.min(jnp.where(selm == m1, erow, n_exp), axis=0, keepdims=True)
    oh1 = erow == i1
    selm2 = jnp.where(oh1, -jnp.inf, selm)
    m2 = jnp.max(selm2, axis=0, keepdims=True)
    i2 = jnp.min(jnp.where(selm2 == m2, erow, n_exp), axis=0, keepdims=True)
    oh2 = erow == i2
    s1 = jnp.sum(jnp.where(oh1, scores, 0.0), axis=0, keepdims=True)
    s2 = jnp.sum(jnp.where(oh2, scores, 0.0), axis=0, keepdims=True)
    tot = s1 + s2
    w1, w2 = s1 / tot, s2 / tot

    oh = jnp.where(oh1 | oh2, 1.0, 0.0)
    tri = (lax.broadcasted_iota(jnp.int32, (tm, tm), 0)
           <= lax.broadcasted_iota(jnp.int32, (tm, tm), 1)).astype(BF16)
    before = _dot(oh.astype(BF16), tri) - oh + run_ref[:, 0:1]
    r1 = jnp.sum(jnp.where(oh1, before, 0.0), axis=0, keepdims=True)
    r2 = jnp.sum(jnp.where(oh2, before, 0.0), axis=0, keepdims=True)
    run = run_ref[...] + jnp.sum(oh, axis=1, keepdims=True)
    run_ref[...] = run
    cnt_ref[...] = run
    zero = jnp.zeros((1, tm), F32)
    route_ref[...] = jnp.concatenate([i1.astype(F32), i2.astype(F32), r1, r2, w1, w2, zero, zero], axis=0)
    wt_ref[...] = jnp.concatenate([w1, w2, jnp.zeros((LANE - 2, tm), F32)], axis=0).T


def _mixer_out(gn, attn, x, mods, g_mix_a, w_out_b, g_norm_ffn, w_router_t, b_router, cnt_in, *, sample, t_len,
               tm):
    n, d = x.shape
    gw = gn.shape[1]
    n_exp = w_router_t.shape[0]
    full = lambda a: pl.BlockSpec(a.shape, lambda i: (0,) * a.ndim)
    row = lambda w: pl.BlockSpec((tm, w), lambda i: (i, 0))
    if sample:
        mod_spec = lambda k: pl.BlockSpec((tm, d), lambda i, k=k: (i, k))
    else:
        tpb = t_len // tm
        mod_spec = lambda k: pl.BlockSpec((None, 1, d), lambda i, k=k: (i // tpb, 0, k))
    return pl.pallas_call(
        functools.partial(_mixer_out_kernel, n_exp),
        out_shape=[jax.ShapeDtypeStruct((n, d), F32), jax.ShapeDtypeStruct((n, d), F32),
                   jax.ShapeDtypeStruct((8, n), F32), jax.ShapeDtypeStruct((n, LANE), F32),
                   jax.ShapeDtypeStruct((n_exp, LANE), F32)],
        grid=(n // tm,),
        in_specs=[row(gw), row(attn.shape[1]), row(d), mod_spec(2), mod_spec(3), mod_spec(4), full(g_mix_a),
                  full(w_out_b), full(g_norm_ffn), full(w_router_t), full(b_router), full(cnt_in)],
        out_specs=[row(d), row(d), pl.BlockSpec((8, tm), lambda i: (0, i)), row(LANE),
                   pl.BlockSpec((n_exp, LANE), lambda i: (0, 0))],
        scratch_shapes=[pltpu.VMEM((n_exp, LANE), F32)],
        compiler_params=_cparams(("arbitrary",)),
        name="mixer_out_sample" if sample else "mixer_out_prompt",
    )(gn, attn, x, mods, mods, mods, g_mix_a, w_out_b, g_norm_ffn, w_router_t, b_router, cnt_in)


def _dispatch_kernel(tm, h_ref, dest_hbm, xs_in, xs_hbm, dsm, sem_d, sem_r):
    del xs_in
    i = pl.program_id(0)
    cp = pltpu.make_async_copy(dest_hbm.at[i], dsm, sem_d)
    cp.start()
    cp.wait()

    def row_copy(r, k):
        return pltpu.make_async_copy(h_ref.at[pl.ds(r, 1)], xs_hbm.at[pl.ds(dsm[k * tm + r], 1)], sem_r)

    for r in range(tm):
        row_copy(r, 0).start(priority=0)
        row_copy(r, 1).start(priority=1)
    for r in range(tm):
        row_copy(r, 0).wait()
        row_copy(r, 1).wait()


def _dispatch(h2, dest, xs, *, tm):
    n, d = h2.shape
    return pl.pallas_call(
        functools.partial(_dispatch_kernel, tm),
        out_shape=jax.ShapeDtypeStruct(xs.shape, xs.dtype),
        grid=(n // tm,),
        in_specs=[pl.BlockSpec((tm, d), lambda i: (i, 0)), pl.BlockSpec(memory_space=pl.ANY),
                  pl.BlockSpec(memory_space=pl.ANY)],
        out_specs=pl.BlockSpec(memory_space=pl.ANY),
        scratch_shapes=[pltpu.SMEM((2 * tm,), jnp.int32), pltpu.SemaphoreType.DMA, pltpu.SemaphoreType.DMA],
        input_output_aliases={2: 0},
        compiler_params=_cparams(("arbitrary",)),
        name="moe_dispatch",
    )(h2, dest, xs)


def _experts_kernel(te_ref, nu_ref, x_ref, wg_ref, wu_ref, wd_ref, y_ref):
    j = pl.program_id(0)

    @pl.when(j < nu_ref[0])
    def _():
        x = x_ref[...].astype(BF16)
        a = _dot(x, wg_ref[...].astype(BF16))
        b = _dot(x, wu_ref[...].astype(BF16))
        act = (a * jax.nn.sigmoid(a)) * b
        y_ref[...] = _dot(act.astype(BF16), wd_ref[...].astype(BF16))

    @pl.when(j >= nu_ref[0])
    def _():
        y_ref[...] = jnp.zeros(y_ref.shape, y_ref.dtype)


def _experts(tile_expert, n_used, xs, w_gate, w_up, w_down, *, layer, tm):
    p, d = xs.shape
    f = w_gate.shape[-1]
    x_spec = pl.BlockSpec((tm, d), lambda j, te, nu: (jnp.minimum(j, nu[0] - 1), 0))
    w_spec = lambda a, b: pl.BlockSpec((None, None, a, b), lambda j, te, nu: (layer, te[j], 0, 0))
    return pl.pallas_call(
        _experts_kernel,
        out_shape=jax.ShapeDtypeStruct((p, d), F32),
        grid_spec=pltpu.PrefetchScalarGridSpec(
            num_scalar_prefetch=2, grid=(p // tm,),
            in_specs=[x_spec, w_spec(d, f), w_spec(d, f), w_spec(f, d)],
            out_specs=pl.BlockSpec((tm, d), lambda j, te, nu: (j, 0))),
        compiler_params=_cparams(("arbitrary",)),
        name="moe_experts",
    )(tile_expert, n_used, xs, w_gate, w_up, w_down)


def _combine_kernel(tm, x1_ref, gt_ref, wt_ref, dest_hbm, ys_hbm, o_ref, dsm, ybuf, sem_d, sem_r):
    i = pl.program_id(0)
    cp = pltpu.make_async_copy(dest_hbm.at[i], dsm, sem_d)
    cp.start()
    cp.wait()

    def row_copy(r, k):
        return pltpu.make_async_copy(ys_hbm.at[pl.ds(dsm[k * tm + r], 1)], ybuf.at[k, pl.ds(r, 1)], sem_r)

    for r in range(tm):
        row_copy(r, 0).start(priority=0)
        row_copy(r, 1).start(priority=1)
    for r in range(tm):
        row_copy(r, 0).wait()
        row_copy(r, 1).wait()
    wt = wt_ref[...]
    y = wt[:, 0:1] * ybuf[0] + wt[:, 1:2] * ybuf[1]
    o_ref[...] = x1_ref[...] + gt_ref[...] * y


def _combine(x1, mods, wt, dest, ys, *, sample, t_len, tm):
    n, d = x1.shape
    row = lambda w: pl.BlockSpec((tm, w), lambda i: (i, 0))
    if sample:
        mod_spec = pl.BlockSpec((tm, d), lambda i: (i, 5))
    else:
        tpb = t_len // tm
        mod_spec = pl.BlockSpec((None, 1, d), lambda i: (i // tpb, 0, 5))
    return pl.pallas_call(
        functools.partial(_combine_kernel, tm),
        out_shape=jax.ShapeDtypeStruct((n, d), F32),
        grid=(n // tm,),
        in_specs=[row(d), mod_spec, row(LANE), pl.BlockSpec(memory_space=pl.ANY),
                  pl.BlockSpec(memory_space=pl.ANY)],
        out_specs=row(d),
        scratch_shapes=[pltpu.SMEM((2 * tm,), jnp.int32), pltpu.VMEM((2, tm, d), F32),
                        pltpu.SemaphoreType.DMA, pltpu.SemaphoreType.DMA],
        compiler_params=_cparams(("arbitrary",)),
        name="moe_combine_sample" if sample else "moe_combine_prompt",
    )(x1, mods, wt, dest, ys)


def _route_tables(counts, routes, tms, tm_e, n_tiles):
    n_exp = counts.shape[0]
    cnt = counts.astype(jnp.int32)
    padded = ((cnt + tm_e - 1) // tm_e) * tm_e
    ends = jnp.cumsum(padded)
    off = ends - padded
    n_used = (ends[-1] // tm_e).reshape(1)
    tile_start = jnp.arange(n_tiles, dtype=jnp.int32) * tm_e
    tile_expert = jnp.minimum(jnp.sum(tile_start[:, None] >= ends[None, :], axis=1), n_exp - 1).astype(jnp.int32)
    eids = jnp.arange(n_exp, dtype=jnp.int32)[:, None]
    dests = []
    for route, tm in zip(routes, tms):
        n = route.shape[1]
        e = route[0:2].astype(jnp.int32)
        base = jnp.sum(jnp.where(e[:, None, :] == eids[None], off[None, :, None], 0), axis=1)
        dest = base + route[2:4].astype(jnp.int32)
        dests.append(dest.reshape(2, n // tm, tm).transpose(1, 0, 2).reshape(n // tm, 2 * tm))
    return tile_expert, n_used, dests


def _rope_tables(pos, rope):
    half = rope // 2
    freqs = ROPE_THETA ** (-jnp.arange(half, dtype=F32) / half)
    ang = pos.astype(F32)[:, None] * freqs[None, :]
    cos, sin = jnp.cos(ang), jnp.sin(ang)
    t = pos.shape[0]
    pad = LANE - QK_NOPE - rope
    cq = jnp.concatenate([jnp.ones((t, QK_NOPE), F32), cos, cos, jnp.zeros((t, pad), F32)], axis=-1)
    sq = jnp.concatenate([jnp.zeros((t, QK_NOPE), F32), sin, sin, jnp.zeros((t, pad), F32)], axis=-1)
    return cq, sq


def _rot_cols(w, rope):
    half = rope // 2
    return jnp.concatenate([-w[..., half:], w[..., :half]], axis=-1)


def _prep_layer(l, w_in, w_uq, w_ukv, g_qk_q, g_qk_k, rope, gw, qrank, kvrank, vhead):
    d = w_in.shape[1]
    qk_head = QK_NOPE + rope
    pad = LANE - qk_head
    o = 2 * gw + qrank + kvrank
    w_kr = w_in[l][:, o:o + rope]
    z64 = jnp.zeros((d, QK_NOPE), F32)
    zp = jnp.zeros((d, pad), F32)
    w_in_p = jnp.concatenate([w_in[l][:, :o], z64, w_kr, zp, z64, _rot_cols(w_kr, rope), zp], axis=-1).astype(BF16)
    wq = w_uq[l].reshape(qrank, M_HEADS, qk_head)
    zq = jnp.zeros((qrank, M_HEADS, pad), F32)
    wq_plain = jnp.concatenate([wq, zq], axis=-1).reshape(qrank, M_HEADS * LANE)
    wq_rot = jnp.concatenate([jnp.zeros((qrank, M_HEADS, QK_NOPE), F32), _rot_cols(wq[..., QK_NOPE:], rope), zq],
                             axis=-1).reshape(qrank, M_HEADS * LANE)
    w_uq_p = jnp.concatenate([wq_plain, wq_rot], axis=-1).astype(BF16)
    wkv = w_ukv[l].reshape(kvrank, M_HEADS, QK_NOPE + vhead)
    wk = wkv[..., :QK_NOPE]
    wk_pad = jnp.concatenate([wk, jnp.zeros((kvrank, M_HEADS, LANE - QK_NOPE), F32)], axis=-1)
    wv = wkv[..., QK_NOPE:].reshape(kvrank, M_HEADS * vhead)
    w_kv_p = jnp.concatenate([wk_pad.reshape(kvrank, M_HEADS * LANE), wv], axis=-1).astype(BF16)
    wukt_pad = jnp.concatenate([wk.transpose(1, 2, 0), jnp.zeros((M_HEADS, LANE - QK_NOPE, kvrank), F32)],
                               axis=1).astype(BF16)
    wukt = wk.transpose(1, 2, 0).reshape(M_HEADS * QK_NOPE, kvrank).astype(BF16)
    zg = jnp.zeros((pad,), F32)
    gq_p = (jnp.concatenate([g_qk_q[l], zg]) * (qk_head ** -0.5)).reshape(1, LANE)
    gk_p = jnp.concatenate([g_qk_k[l], zg]).reshape(1, LANE)
    return w_in_p, w_uq_p, w_kv_p, wukt_pad, wukt, wv.astype(BF16), gq_p, gk_p


def kernel(x_prompt, x_sample, c_prompt, c_sample, cache_kv_latent, cache_k_rope, page_table, w_ada, b_ada,
           g_norm_mix, g_norm_ffn, w_in, g_v, w_s, b_s, g_cq, w_uq, g_ckv, w_ukv, g_qk_q, g_qk_k, g_mix_out,
           w_out, w_router, b_router, w_gate, w_up, w_down):
    batch, t_p, d = x_prompt.shape
    s_n, t_s, _ = x_sample.shape
    depth = w_ada.shape[0]
    gheads, chunk = w_s.shape[1], w_s.shape[2]
    hd = g_v.shape[-1]
    gw = gheads * hd
    qrank = g_cq.shape[-1]
    kvrank = g_ckv.shape[-1]
    rope = cache_k_rope.shape[-1]
    vhead = w_ukv.shape[-1] // M_HEADS - QK_NOPE
    past = page_table.shape[1] * cache_kv_latent.shape[2]
    assert t_s == 1 and t_p % chunk == 0 and g_qk_q.shape[-1] == QK_NOPE + rope
    row2 = lambda a: a.reshape(1, -1)

    mods = _ada(jnp.concatenate([c_prompt, c_sample], axis=0), w_ada, b_ada)
    tab_p = _rope_tables(jnp.arange(t_p, dtype=jnp.int32), rope)
    tab_s = _rope_tables(past + jnp.arange(t_s, dtype=jnp.int32), rope)
    w_router_t = w_router.T
    b_router_c = b_router.reshape(-1, 1)

    tm_p = 256
    tm_s = s_n
    n_exp = w_router.shape[1]
    tm_e = 512
    n_pairs = 2 * (batch * t_p + s_n)
    n_tiles_e = -(-n_pairs // tm_e) + n_exp
    tq = 512 if t_p % 512 == 0 else 256
    cache_krt = jnp.swapaxes(cache_k_rope, 2, 3)
    xp = x_prompt.reshape(batch * t_p, d)
    xs = x_sample.reshape(s_n, d)
    open_p = ((t_p - 1) // chunk) * chunk
    outs = [[] for _ in range(6)]
    for l in range(depth):
        w_in_p, w_uq_p, w_kv_p, wukt_pad, wukt, wuv, gq_p, gk_p = _prep_layer(
            l, w_in, w_uq, w_ukv, g_qk_q, g_qk_k, rope, gw, qrank, kvrank, vhead)
        g_mix_g, g_mix_a = row2(g_mix_out[l, :gw]), row2(g_mix_out[l, gw:])
        w_out_b = w_out[l].astype(BF16)
        base = (row2(g_norm_mix[l]), w_in_p, row2(g_v[l]), row2(g_cq[l]), w_uq_p, row2(g_ckv[l]), w_kv_p,
                gq_p, gk_p, g_mix_g)
        mp = mods[l, :batch].reshape(batch, 1, N_MOD * d)
        ms = mods[l, batch:]

        w00 = row2(jnp.repeat(w_s[l, :, 0, 0], hd))
        b0 = row2(jnp.repeat(b_s[l, :, 0], hd))
        gn, v, q, ckv, kr, k, qa = _mixer_in(xs, ms, base + ((w00, b0, wukt_pad),), tab_s,
                                             sample=True, t_len=t_s, tm=tm_s, rope=rope)
        attn = _sample_attn(page_table, q, qa, k, ckv, gk_p, wukt, wuv, cache_kv_latent, cache_krt, layer=l)
        x1s, h2s, route_s, wt_s, cnt = _mixer_out(gn, attn, xs, ms, g_mix_a, w_out_b, row2(g_norm_ffn[l]),
                                                  w_router_t, b_router_c, jnp.zeros((n_exp, LANE), F32),
                                                  sample=True, t_len=t_s, tm=tm_s)
        outs[3].append(ckv.reshape(s_n, t_s, kvrank))
        outs[4].append(kr.reshape(s_n, t_s, rope))
        outs[5].append(v.reshape(s_n, t_s, gheads, hd))

        bs_full = jnp.broadcast_to(b_s[l][:, :, None], (gheads, chunk, hd))
        gn, v, q, ckv, kr, k, vv = _mixer_in(xp, mp, base + ((w_s[l], bs_full),), tab_p,
                                             sample=False, t_len=t_p, tm=tm_p, rope=rope)
        attn = _prompt_attn(q, k, vv, batch=batch, t_len=t_p, tq=tq)
        x1p, h2p, route_p, wt_p, cnt = _mixer_out(gn, attn, xp, mp, g_mix_a, w_out_b, row2(g_norm_ffn[l]),
                                                  w_router_t, b_router_c, cnt, sample=False, t_len=t_p, tm=tm_p)
        outs[0].append(ckv.reshape(batch, t_p, kvrank))
        outs[1].append(kr.reshape(batch, t_p, rope))
        outs[2].append(v.reshape(batch, t_p, gheads, hd)[:, open_p:])

        tile_expert, n_used, (dest_s, dest_p) = _route_tables(cnt[:, 0], (route_s, route_p), (tm_s, tm_p), tm_e,
                                                              n_tiles_e)
        xsort = jnp.zeros((n_tiles_e * tm_e, d), F32)
        xsort = _dispatch(h2s, dest_s, xsort, tm=tm_s)
        xsort = _dispatch(h2p, dest_p, xsort, tm=tm_p)
        ysort = _experts(tile_expert, n_used, xsort, w_gate, w_up, w_down, layer=l, tm=tm_e)
        xs = _combine(x1s, ms, wt_s, dest_s, ysort, sample=True, t_len=t_s, tm=tm_s)
        xp = _combine(x1p, mp, wt_p, dest_p, ysort, sample=False, t_len=t_p, tm=tm_p)

    return (xp.reshape(batch, t_p, d), xs.reshape(s_n, t_s, d), jnp.stack(outs[0]), jnp.stack(outs[1]),
            jnp.stack(outs[2]), jnp.stack(outs[3]), jnp.stack(outs[4]), jnp.stack(outs[5]))
```

```python
import functools

import jax
import jax.numpy as jnp
from jax import lax
from jax.experimental import pallas as pl
from jax.experimental.pallas import tpu as pltpu

F32 = jnp.float32
BF16 = jnp.bfloat16

M_HEADS = 8
QK_NOPE = 64
N_GROUPS = 4
N_MOD = 6
ROPE_THETA = 10000.0
EPS = 1e-6
LANE = 128
MXU_TILE = 256
VMEM_LIMIT = 56 * 1024 * 1024


def _cparams(sem):
    return pltpu.CompilerParams(dimension_semantics=sem, vmem_limit_bytes=VMEM_LIMIT)


def _dot(a, b):
    return jnp.dot(a, b, preferred_element_type=F32)


def _dot_nt(a, b, precision=None):
    return lax.dot_general(a, b, (((1,), (1,)), ((), ())), preferred_element_type=F32,
                           precision=precision)


def _load_row_tiles(ref, rows):
    k = ref.shape[0] // rows
    return jnp.concatenate([ref[pl.ds(c, rows, stride=k), :] for c in range(k)], axis=-1)


def _store_row_tiles(ref, x):
    rows = x.shape[0]
    k = ref.shape[0] // rows
    for c in range(k):
        ref[pl.ds(c, rows, stride=k), :] = x[:, c * LANE:(c + 1) * LANE]


def _rms(x, g, n=None):
    n = x.shape[-1] if n is None else n
    ms = jnp.sum(x * x, axis=-1, keepdims=True) * (1.0 / n)
    return x * lax.rsqrt(ms + EPS) * g


def _ada_kernel(c_ref, w_ref, b_ref, o_ref):
    c = c_ref[...]
    s = c * jax.nn.sigmoid(c)
    o_ref[0] = _dot(s.astype(BF16), w_ref[0].astype(BF16)) + b_ref[0]


def _ada(c_all, w_ada, b_ada):
    depth, d, n6 = w_ada.shape
    rows = c_all.shape[0]
    tn = 1536
    return pl.pallas_call(
        _ada_kernel,
        out_shape=jax.ShapeDtypeStruct((depth, rows, n6), F32),
        grid=(depth, n6 // tn),
        in_specs=[pl.BlockSpec((rows, d), lambda l, j: (0, 0)),
                  pl.BlockSpec((1, d, tn), lambda l, j: (l, 0, j)),
                  pl.BlockSpec((1, 1, tn), lambda l, j: (l, 0, j))],
        out_specs=pl.BlockSpec((1, rows, tn), lambda l, j: (l, 0, j)),
        compiler_params=_cparams(("arbitrary", "arbitrary")),
        name="ada_modulation",
    )(c_all, w_ada, b_ada.reshape(depth, 1, n6))


def _mixer_in_kernel(dims, sample, *refs):
    gw, qrank, kvrank, gheads, vw = dims
    (x_ref, sh_ref, sc_ref, gnm_ref, win_ref, gv_ref, gcq_ref, wuq_ref, gckv_ref, wkv_ref,
     gq_ref, gk_ref, cq_ref, sq_ref, gmix_ref) = refs[:15]
    if sample:
        w00_ref, b0_ref, wukt_ref = refs[15:18]
        gn_ref, v_ref, q_ref, ckv_ref, kr_ref, k_ref, qa_ref = refs[18:]
    else:
        ws_ref, bs_ref = refs[15:17]
        gn_ref, v_ref, q_ref, ckv_ref, kr_ref, k_ref, vv_ref = refs[17:]

    x = x_ref[...]
    h = _rms(x, gnm_ref[...]) * (1.0 + sc_ref[...]) + sh_ref[...]
    z = _dot(h.astype(BF16), win_ref[...])
    tm = z.shape[0]
    u = jax.nn.gelu(z[:, :gw])
    vg = jax.nn.gelu(z[:, gw:2 * gw])
    hd = gw // gheads
    gv = gv_ref[...]
    v = jnp.concatenate([_rms(vg[:, i * hd:(i + 1) * hd], gv) for i in range(gheads)], axis=-1)
    v_ref[...] = v

    if sample:
        s = v * w00_ref[...] + b0_ref[...]
    else:
        chunk = ws_ref.shape[-1]
        row = lax.broadcasted_iota(jnp.int32, (chunk, chunk), 0)
        col = lax.broadcasted_iota(jnp.int32, (chunk, chunk), 1)
        vb = v.astype(BF16)
        cols = []
        for i in range(gheads):
            wt = jnp.where(col <= row, ws_ref[i], 0.0).astype(BF16)
            rows = [_dot(wt, vb[c * chunk:(c + 1) * chunk, i * hd:(i + 1) * hd]) + bs_ref[i]
                    for c in range(tm // chunk)]
            cols.append(jnp.concatenate(rows, axis=0) if len(rows) > 1 else rows[0])
        s = jnp.concatenate(cols, axis=-1)
    g = u * s
    gn_ref[...] = _rms(g, gmix_ref[...]).astype(gn_ref.dtype)

    o0 = 2 * gw
    cq = _rms(z[:, o0:o0 + qrank], gcq_ref[...])
    qq = _dot(cq.astype(BF16), wuq_ref[...])
    hw = M_HEADS * LANE
    cqt = cq_ref[...]
    sqt = sq_ref[...]
    gq = gq_ref[...]
    n_real = QK_NOPE + kr_ref.shape[-1]
    q_heads = []
    for i in range(M_HEADS):
        qh = qq[:, i * LANE:(i + 1) * LANE] * cqt + qq[:, hw + i * LANE:hw + (i + 1) * LANE] * sqt
        q_heads.append(_rms(qh, gq, n_real))
    q = jnp.concatenate(q_heads, axis=-1)
    q_ref[...] = q.astype(q_ref.dtype)

    o1 = o0 + qrank
    ckv = _rms(z[:, o1:o1 + kvrank], gckv_ref[...])
    ckv_ref[...] = ckv
    o2 = o1 + kvrank
    krp = z[:, o2:o2 + LANE] * cqt + z[:, o2 + LANE:o2 + 2 * LANE] * sqt
    rope = kr_ref.shape[-1]
    kr_ref[...] = krp[:, QK_NOPE:QK_NOPE + rope]
    kv = _dot(ckv.astype(BF16), wkv_ref[...])
    kr_ss = jnp.sum(krp * krp, axis=-1, keepdims=True)
    gk = gk_ref[...]
    k_heads = []
    for i in range(M_HEADS):
        kn = kv[:, i * LANE:(i + 1) * LANE]
        ss = jnp.sum(kn * kn, axis=-1, keepdims=True) + kr_ss
        k_heads.append((kn + krp) * lax.rsqrt(ss * (1.0 / n_real) + EPS) * gk)
    k_ref[...] = jnp.concatenate(k_heads, axis=-1).astype(k_ref.dtype)
    if sample:
        lane = lax.broadcasted_iota(jnp.int32, (1, LANE), 1)
        gk_nope = jnp.where(lane < QK_NOPE, gk, 0.0)
        qa = [_dot((q_heads[i] * gk_nope).astype(BF16), wukt_ref[i]) for i in range(M_HEADS)]
        qa_ref[...] = jnp.concatenate(qa, axis=-1)
    else:
        vv_ref[...] = kv[:, hw:hw + vw].astype(vv_ref.dtype)


def _mixer_in(x, mods, lw, tabs, *, sample, t_len, tm, rope):
    n, d = x.shape
    (g_norm_mix, w_in_p, g_v, g_cq, w_uq_p, g_ckv, w_kv_p, gq_p, gk_p, g_mix_g, extra) = lw
    cq_tab, sq_tab = tabs
    gw = g_mix_g.shape[-1]
    qrank = g_cq.shape[-1]
    kvrank = g_ckv.shape[-1]
    gheads = gw // g_v.shape[-1]
    assert w_in_p.shape[1] == 2 * gw + qrank + kvrank + 2 * LANE
    vw = w_kv_p.shape[1] - M_HEADS * LANE
    nt = n // tm
    full = lambda a: pl.BlockSpec(a.shape, lambda i: (0,) * a.ndim)
    if sample:
        mod_spec = lambda k: pl.BlockSpec((tm, d), lambda i, k=k: (i, k))
        tab_spec = pl.BlockSpec((1, LANE), lambda i: (0, 0))
    else:
        tpb = t_len // tm
        mod_spec = lambda k: pl.BlockSpec((None, 1, d), lambda i, k=k: (i // tpb, 0, k))
        tab_spec = pl.BlockSpec((tm, LANE), lambda i: (i % tpb, 0))
    row = lambda w: pl.BlockSpec((tm, w), lambda i: (i, 0))
    in_specs = [row(d), mod_spec(0), mod_spec(1), full(g_norm_mix), full(w_in_p), full(g_v), full(g_cq),
                full(w_uq_p), full(g_ckv), full(w_kv_p), full(gq_p), full(gk_p), tab_spec, tab_spec,
                full(g_mix_g)] + [full(a) for a in extra]
    kr_w = rope
    hw = M_HEADS * LANE
    if sample:
        out_shape = [jax.ShapeDtypeStruct((n, gw), BF16), jax.ShapeDtypeStruct((n, gw), F32),
                     jax.ShapeDtypeStruct((n, hw), F32), jax.ShapeDtypeStruct((n, kvrank), F32),
                     jax.ShapeDtypeStruct((n, kr_w), F32), jax.ShapeDtypeStruct((n, hw), F32),
                     jax.ShapeDtypeStruct((n, M_HEADS * kvrank), F32)]
        out_specs = [row(gw), row(gw), row(hw), row(kvrank), row(kr_w), row(hw), row(M_HEADS * kvrank)]
    else:
        out_shape = [jax.ShapeDtypeStruct((n, gw), BF16), jax.ShapeDtypeStruct((n, gw), F32),
                     jax.ShapeDtypeStruct((n, hw), BF16), jax.ShapeDtypeStruct((n, kvrank), F32),
                     jax.ShapeDtypeStruct((n, kr_w), F32), jax.ShapeDtypeStruct((n, hw), BF16),
                     jax.ShapeDtypeStruct((n, vw), BF16)]
        out_specs = [row(gw), row(gw), row(hw), row(kvrank), row(kr_w), row(hw), row(vw)]
    dims = (gw, qrank, kvrank, gheads, vw)
    return pl.pallas_call(
        functools.partial(_mixer_in_kernel, dims, sample),
        out_shape=out_shape,
        grid=(nt,),
        in_specs=in_specs,
        out_specs=out_specs,
        compiler_params=_cparams(("arbitrary",)),
        name="mixer_in_sample" if sample else "mixer_in_prompt",
    )(x, mods, mods, g_norm_mix, w_in_p, g_v, g_cq, w_uq_p, g_ckv, w_kv_p, gq_p, gk_p, cq_tab, sq_tab,
      g_mix_g, *extra)


def _prompt_attn_kernel(tq, vhead, q_ref, k_ref, v_ref, o_ref):
    qi = pl.program_id(2)
    heads = q_ref.shape[-1] // LANE
    qs = [q_ref[:, h * LANE:(h + 1) * LANE] for h in range(heads)]

    def scores(j, h):
        start = pl.multiple_of(j * tq, tq)
        k = k_ref[pl.ds(start, tq), h * LANE:(h + 1) * LANE]
        v = v_ref[pl.ds(start, tq), h * vhead:(h + 1) * vhead]
        return _dot_nt(qs[h], k), v

    def update(carry, s, v):
        m, l, acc = carry
        m_new = jnp.maximum(m, jnp.max(s, axis=-1, keepdims=True))
        alpha = jnp.exp(m - m_new)
        p = jnp.exp(s - m_new)
        l = l * alpha + jnp.sum(p, axis=-1, keepdims=True)
        acc = acc * alpha + _dot(p.astype(BF16), v)
        return m_new, l, acc

    def body(j, carry):
        return tuple(update(carry[h], *scores(j, h)) for h in range(heads))

    init = tuple((jnp.full((tq, 1), -jnp.inf, F32), jnp.zeros((tq, 1), F32), jnp.zeros((tq, vhead), F32))
                 for _ in range(heads))
    carry = lax.fori_loop(0, qi, body, init)
    causal = (lax.broadcasted_iota(jnp.int32, (tq, tq), 1) <= lax.broadcasted_iota(jnp.int32, (tq, tq), 0))
    outs = []
    for h in range(heads):
        s, v = scores(qi, h)
        m, l, acc = update(carry[h], jnp.where(causal, s, -jnp.inf), v)
        outs.append(acc / l)
    o_ref[...] = jnp.concatenate(outs, axis=-1).astype(o_ref.dtype)


def _prompt_attn(q, k, v, *, batch, t_len, tq):
    n = q.shape[0]
    vhead = v.shape[1] // M_HEADS
    hp = LANE // vhead
    nq = t_len // tq
    return pl.pallas_call(
        functools.partial(_prompt_attn_kernel, tq, vhead),
        out_shape=jax.ShapeDtypeStruct((n, v.shape[1]), F32),
        grid=(batch, M_HEADS // hp, nq),
        in_specs=[pl.BlockSpec((tq, hp * LANE), lambda b, h, i: (b * nq + i, h)),
                  pl.BlockSpec((t_len, hp * LANE), lambda b, h, i: (b, h)),
                  pl.BlockSpec((t_len, hp * vhead), lambda b, h, i: (b, h))],
        out_specs=pl.BlockSpec((tq, hp * vhead), lambda b, h, i: (b * nq + i, h)),
        compiler_params=_cparams(("arbitrary", "arbitrary", "arbitrary")),
        name="prompt_attention",
    )(q, k, v)


def _sample_attn_kernel(layer, n_pages, ppt, tpi, pt_ref, q_ref, qa_ref, knew_ref, cnew_ref, gk_ref, wukt_ref,
                        wuv_ref, ckv_hbm, krt_hbm, o_ref, cbuf, rbuf, sem_c, sem_r, wext_ref, s_scr, p_scr,
                        kt_a, kt_b):
    seq = pl.program_id(0)
    n_seq = pl.num_programs(0)
    slot = lax.rem(seq, 2)
    nxt_slot = 1 - slot
    nxt = jnp.minimum(seq + 1, n_seq - 1)
    rope, page = rbuf.shape[2], rbuf.shape[3]
    tile = ppt * page
    n_tiles = n_pages // ppt
    hn = wukt_ref.shape[0]
    n_real = QK_NOPE + rope

    def page_copies(sq, sl, i):
        pg = pt_ref[sq * n_pages + i]
        return (pltpu.make_async_copy(ckv_hbm.at[layer, pg], cbuf.at[sl, pl.ds(i * page, page)], sem_c.at[sl]),
                pltpu.make_async_copy(krt_hbm.at[layer, pg], rbuf.at[sl, i], sem_r.at[sl]))

    def start_page(sq, sl, i):
        for cp in page_copies(sq, sl, i):
            cp.start()

    def wait_pages(sq, sl):
        def body(i, carry):
            for cp in page_copies(sq, sl, i):
                cp.wait()
            return carry
        lax.fori_loop(0, n_pages, body, 0)

    @pl.when(seq == 0)
    def _():
        def body(i, carry):
            start_page(0, 0, i)
            return carry
        lax.fori_loop(0, n_pages, body, 0)

    wait_pages(seq, slot)

    q = q_ref[0]
    pad = wext_ref.shape[0] - hn - M_HEADS
    wext_ref[...] = jnp.concatenate(
        [wukt_ref[...], qa_ref[0].astype(BF16), jnp.zeros((pad, wext_ref.shape[1]), BF16)], axis=0)
    qr = (q * gk_ref[...])[:, QK_NOPE:QK_NOPE + rope].astype(BF16)

    def c_tile(t):
        return cbuf[slot, pl.ds(pl.multiple_of(t * tile, tile), tile), :].astype(BF16)

    def expand(g, kt_ref):
        for u in range(tpi):
            kt_ref[u] = _dot_nt(wext_ref[...], c_tile(g * tpi + u))

    def score(g, kt_ref):
        for u in range(tpi):
            t = g * tpi + u
            for k in range(ppt):
                start_page(nxt, nxt_slot, t * ppt + k)
            krt = jnp.concatenate([rbuf[slot, t * ppt + k] for k in range(ppt)], axis=1)
            kn = kt_ref[u, :hn, :]
            ss = jnp.sum((kn * kn).reshape(M_HEADS, QK_NOPE, tile), axis=1)
            kr_ss = jnp.sum(krt * krt, axis=0, keepdims=True)
            rinv = lax.rsqrt((ss + kr_ss) * (1.0 / n_real) + EPS)
            s_scr[t] = (kt_ref[u, hn:hn + M_HEADS, :] + _dot(qr, krt.astype(BF16))) * rinv

    n_groups = n_tiles // tpi
    n_pairs = (n_groups - 1) // 2
    expand(0, kt_a)

    def pair(h, carry):
        g = 2 * h
        expand(g + 1, kt_b)
        score(g, kt_a)
        expand(g + 2, kt_a)
        score(g + 1, kt_b)
        return carry

    lax.fori_loop(0, n_pairs, pair, 0)
    g_tail = 2 * n_pairs
    if n_groups - g_tail == 2:
        expand(g_tail + 1, kt_b)
        score(g_tail, kt_a)
        score(g_tail + 1, kt_b)
    else:
        score(g_tail, kt_a)

    s_all = s_scr[...]
    s_new = jnp.sum(q * knew_ref[0], axis=-1, keepdims=True)
    m = jnp.maximum(jnp.max(jnp.max(s_all, axis=0), axis=-1, keepdims=True), s_new)
    p_all = jnp.exp(s_all - m)
    p_new = jnp.exp(s_new - m)
    l = jnp.sum(jnp.sum(p_all, axis=0), axis=-1, keepdims=True) + p_new
    p_scr[...] = p_all

    def phase_c(g, accs):
        return tuple(accs[u] + _dot(p_scr[g * tpi + u].astype(BF16), c_tile(g * tpi + u)) for u in range(tpi))

    zero = jnp.zeros((M_HEADS, cbuf.shape[-1]), F32)
    accs = lax.fori_loop(0, n_tiles // tpi, phase_c, (zero,) * tpi)
    acc = functools.reduce(lambda a, b: a + b, accs)
    c_new = cnew_ref[0].astype(BF16).astype(F32)
    o_lat = (acc + p_new.astype(BF16).astype(F32) * c_new) / l
    full = _dot(o_lat.astype(BF16), wuv_ref[...])
    vhead = full.shape[-1] // M_HEADS
    hrow = lax.broadcasted_iota(jnp.int32, full.shape, 0)
    hcol = lax.broadcasted_iota(jnp.int32, full.shape, 1) // vhead
    o_ref[0] = jnp.sum(jnp.where(hrow == hcol, full, 0.0), axis=0, keepdims=True)

    @pl.when(seq == n_seq - 1)
    def _():
        wait_pages(nxt, nxt_slot)


def _sample_attn(page_table, q, qa, k_new, c_new, gk_p, wukt, wuv, cache_kv, cache_krt, *, layer):
    s_n = q.shape[0]
    n_pages = page_table.shape[1]
    page = cache_kv.shape[2]
    kvrank = cache_kv.shape[3]
    rope = cache_krt.shape[2]
    ppt = max(1, MXU_TILE // page)
    assert n_pages % ppt == 0
    n_tiles = n_pages // ppt
    tpi = 4
    while n_tiles % tpi:
        tpi //= 2
    tile = ppt * page
    vw = wuv.shape[1]
    seq3 = lambda a, b: pl.BlockSpec((1, a, b), lambda s, pt: (s, 0, 0))
    full = lambda a: pl.BlockSpec(a.shape, lambda s, pt: (0,) * a.ndim)
    hbm = pl.BlockSpec(memory_space=pl.ANY)
    in_specs = [seq3(M_HEADS, LANE), seq3(M_HEADS, kvrank), seq3(M_HEADS, LANE), seq3(1, kvrank),
                full(gk_p), full(wukt), full(wuv), hbm, hbm]
    wext_rows = wukt.shape[0] + 2 * M_HEADS
    return pl.pallas_call(
        functools.partial(_sample_attn_kernel, layer, n_pages, ppt, tpi),
        out_shape=jax.ShapeDtypeStruct((s_n, 1, vw), F32),
        grid_spec=pltpu.PrefetchScalarGridSpec(
            num_scalar_prefetch=1, grid=(s_n,), in_specs=in_specs,
            out_specs=pl.BlockSpec((1, 1, vw), lambda s, pt: (s, 0, 0)),
            scratch_shapes=[pltpu.VMEM((2, n_pages * page, kvrank), F32),
                            pltpu.VMEM((2, n_pages, rope, page), F32),
                            pltpu.SemaphoreType.DMA((2,)), pltpu.SemaphoreType.DMA((2,)),
                            pltpu.VMEM((wext_rows, kvrank), BF16),
                            pltpu.VMEM((n_tiles, M_HEADS, tile), F32),
                            pltpu.VMEM((n_tiles, M_HEADS, tile), F32),
                            pltpu.VMEM((tpi, wext_rows, tile), F32),
                            pltpu.VMEM((tpi, wext_rows, tile), F32)]),
        compiler_params=_cparams(("arbitrary",)),
        name="sample_attention",
    )(page_table.reshape(-1), q.reshape(s_n, M_HEADS, LANE), qa.reshape(s_n, M_HEADS, kvrank),
      k_new.reshape(s_n, M_HEADS, LANE), c_new.reshape(s_n, 1, kvrank), gk_p, wukt, wuv,
      cache_kv, cache_krt).reshape(s_n, vw)


def _mixer_out_kernel(n_exp, gn_ref, a_ref, x_ref, gt_ref, sh_ref, sc_ref, gmix_ref, wout_ref, gffn_ref,
                      wr_ref, br_ref, cnt_in_ref, x1_ref, h2_ref, route_ref, wt_ref, cnt_ref, run_ref):
    @pl.when(pl.program_id(0) == 0)
    def _():
        run_ref[...] = cnt_in_ref[...]

    gw = gn_ref.shape[-1]
    an = _rms(a_ref[...].astype(F32), gmix_ref[...])
    y = _dot(gn_ref[...], wout_ref[:gw, :]) + _dot(an.astype(BF16), wout_ref[gw:, :])
    x1 = x_ref[...] + gt_ref[...] * y
    x1_ref[...] = x1
    h2 = _rms(x1, gffn_ref[...]) * (1.0 + sc_ref[...]) + sh_ref[...]
    _store_row_tiles(h2_ref, h2)
    tm = h2.shape[0]

    logits = _dot_nt(wr_ref[...], h2, precision=lax.Precision.HIGHEST)
    scores = jax.nn.sigmoid(logits)
    sel = scores + br_ref[...]
    per = n_exp // N_GROUPS
    best = None
    for g in range(N_GROUPS):
        a, b, c, d = [sel[g * per + i:g * per + i + 1, :] for i in range(per)]
        hi1, lo1 = jnp.maximum(a, b), jnp.minimum(a, b)
        hi2, lo2 = jnp.maximum(c, d), jnp.minimum(c, d)
        gs = jnp.maximum(hi1, hi2) + jnp.maximum(jnp.minimum(hi1, hi2), jnp.maximum(lo1, lo2))
        if best is None:
            best, grp = gs, jnp.zeros(gs.shape, jnp.int32)
        else:
            better = gs > best
            grp = jnp.where(better, g, grp)
            best = jnp.where(better, gs, best)
    erow = lax.broadcasted_iota(jnp.int32, (n_exp, tm), 0)
    selm = jnp.where(erow // per == grp, sel, -jnp.inf)
    m1 = jnp.max(selm, axis=0, keepdims=True)
    i1 = jnp.min(jnp.where(selm == m1, erow, n_exp), axis=0, keepdims=True)
    oh1 = erow == i1
    selm2 = jnp.where(oh1, -jnp.inf, selm)
    m2 = jnp.max(selm2, axis=0, keepdims=True)
    i2 = jnp.min(jnp.where(selm2 == m2, erow, n_exp), axis=0, keepdims=True)
    oh2 = erow == i2
    s1 = jnp.sum(jnp.where(oh1, scores, 0.0), axis=0, keepdims=True)
    s2 = jnp.sum(jnp.where(oh2, scores, 0.0), axis=0, keepdims=True)
    tot = s1 + s2
    w1, w2 = s1 / tot, s2 / tot

    oh = jnp.where(oh1 | oh2, 1.0, 0.0)
    tri = (lax.broadcasted_iota(jnp.int32, (tm, tm), 0)
           <= lax.broadcasted_iota(jnp.int32, (tm, tm), 1)).astype(BF16)
    before = _dot(oh.astype(BF16), tri) - oh + run_ref[:, 0:1]
    r1 = jnp.sum(jnp.where(oh1, before, 0.0), axis=0, keepdims=True)
    r2 = jnp.sum(jnp.where(oh2, before, 0.0), axis=0, keepdims=True)
    run = run_ref[...] + jnp.sum(oh, axis=1, keepdims=True)
    run_ref[...] = run
    cnt_ref[...] = run
    zero = jnp.zeros((1, tm), F32)
    route_ref[...] = jnp.concatenate([i1.astype(F32), i2.astype(F32), r1, r2, w1, w2, zero, zero], axis=0)
    wt_ref[...] = jnp.concatenate([w1, w2, jnp.zeros((LANE - 2, tm), F32)], axis=0).T


def _mixer_out(gn, attn, x, mods, g_mix_a, w_out_b, g_norm_ffn, w_router_t, b_router, cnt_in, *, sample, t_len,
               tm):
    n, d = x.shape
    gw = gn.shape[1]
    n_exp = w_router_t.shape[0]
    full = lambda a: pl.BlockSpec(a.shape, lambda i: (0,) * a.ndim)
    row = lambda w: pl.BlockSpec((tm, w), lambda i: (i, 0))
    if sample:
        mod_spec = lambda k: pl.BlockSpec((tm, d), lambda i, k=k: (i, k))
    else:
        tpb = t_len // tm
        mod_spec = lambda k: pl.BlockSpec((None, 1, d), lambda i, k=k: (i // tpb, 0, k))
    return pl.pallas_call(
        functools.partial(_mixer_out_kernel, n_exp),
        out_shape=[jax.ShapeDtypeStruct((n, d), F32), jax.ShapeDtypeStruct((n * (d // LANE), LANE), F32),
                   jax.ShapeDtypeStruct((8, n), F32), jax.ShapeDtypeStruct((n, LANE), F32),
                   jax.ShapeDtypeStruct((n_exp, LANE), F32)],
        grid=(n // tm,),
        in_specs=[row(gw), row(attn.shape[1]), row(d), mod_spec(2), mod_spec(3), mod_spec(4), full(g_mix_a),
                  full(w_out_b), full(g_norm_ffn), full(w_router_t), full(b_router), full(cnt_in)],
        out_specs=[row(d), pl.BlockSpec((tm * (d // LANE), LANE), lambda i: (i, 0)),
                   pl.BlockSpec((8, tm), lambda i: (0, i)), row(LANE),
                   pl.BlockSpec((n_exp, LANE), lambda i: (0, 0))],
        scratch_shapes=[pltpu.VMEM((n_exp, LANE), F32)],
        compiler_params=_cparams(("arbitrary",)),
        name="mixer_out_sample" if sample else "mixer_out_prompt",
    )(gn, attn, x, mods, mods, mods, g_mix_a, w_out_b, g_norm_ffn, w_router_t, b_router, cnt_in)


def _dispatch_kernel(tm, k, h_ref, dest_hbm, xs_in, xs_hbm, dsm, sem_d, sem_r):
    del xs_in
    i = pl.program_id(0)
    cp = pltpu.make_async_copy(dest_hbm.at[i], dsm, sem_d)
    cp.start()
    cp.wait()

    def row_copy(r, c):
        dst = pl.multiple_of(dsm[c * tm + r] * k, k)
        return pltpu.make_async_copy(h_ref.at[pl.ds(r * k, k)], xs_hbm.at[pl.ds(dst, k)], sem_r)

    for r in range(tm):
        row_copy(r, 0).start(priority=0)
        row_copy(r, 1).start(priority=1)
    for r in range(tm):
        row_copy(r, 0).wait()
        row_copy(r, 1).wait()


def _dispatch(h2t, dest, xs, *, tm, k):
    n = h2t.shape[0] // k
    return pl.pallas_call(
        functools.partial(_dispatch_kernel, tm, k),
        out_shape=jax.ShapeDtypeStruct(xs.shape, xs.dtype),
        grid=(n // tm,),
        in_specs=[pl.BlockSpec((tm * k, LANE), lambda i: (i, 0)), pl.BlockSpec(memory_space=pl.ANY),
                  pl.BlockSpec(memory_space=pl.ANY)],
        out_specs=pl.BlockSpec(memory_space=pl.ANY),
        scratch_shapes=[pltpu.SMEM((2 * tm,), jnp.int32), pltpu.SemaphoreType.DMA, pltpu.SemaphoreType.DMA],
        input_output_aliases={2: 0},
        compiler_params=_cparams(("arbitrary",)),
        name="moe_dispatch",
    )(h2t, dest, xs)


def _experts_kernel(tm, te_ref, nu_ref, x_ref, wg_ref, wu_ref, wd_ref, y_ref):
    j = pl.program_id(0)

    @pl.when(j < nu_ref[0])
    def _():
        x = _load_row_tiles(x_ref, tm).astype(BF16)
        a = _dot(x, wg_ref[...].astype(BF16))
        b = _dot(x, wu_ref[...].astype(BF16))
        act = (a * jax.nn.sigmoid(a)) * b
        _store_row_tiles(y_ref, _dot(act.astype(BF16), wd_ref[...].astype(BF16)))

    @pl.when(j >= nu_ref[0])
    def _():
        y_ref[...] = jnp.zeros(y_ref.shape, y_ref.dtype)


def _experts(tile_expert, n_used, xs, w_gate, w_up, w_down, *, layer, tm):
    d, f = w_gate.shape[-2:]
    k = d // LANE
    p = xs.shape[0] // k
    x_spec = pl.BlockSpec((tm * k, LANE), lambda j, te, nu: (jnp.minimum(j, nu[0] - 1), 0))
    w_spec = lambda a, b: pl.BlockSpec((None, None, a, b), lambda j, te, nu: (layer, te[j], 0, 0))
    return pl.pallas_call(
        functools.partial(_experts_kernel, tm),
        out_shape=jax.ShapeDtypeStruct(xs.shape, F32),
        grid_spec=pltpu.PrefetchScalarGridSpec(
            num_scalar_prefetch=2, grid=(p // tm,),
            in_specs=[x_spec, w_spec(d, f), w_spec(d, f), w_spec(f, d)],
            out_specs=pl.BlockSpec((tm * k, LANE), lambda j, te, nu: (j, 0))),
        compiler_params=_cparams(("arbitrary",)),
        name="moe_experts",
    )(tile_expert, n_used, xs, w_gate, w_up, w_down)


def _combine_kernel(tm, k, x1_ref, gt_ref, wt_ref, dest_hbm, ys_hbm, o_ref, dsm, ybuf0, ybuf1, sem_d, sem_r):
    i = pl.program_id(0)
    cp = pltpu.make_async_copy(dest_hbm.at[i], dsm, sem_d)
    cp.start()
    cp.wait()
    ybufs = (ybuf0, ybuf1)

    def row_copy(r, c):
        src = pl.multiple_of(dsm[c * tm + r] * k, k)
        return pltpu.make_async_copy(ys_hbm.at[pl.ds(src, k)], ybufs[c].at[pl.ds(r * k, k)], sem_r)

    for r in range(tm):
        row_copy(r, 0).start(priority=0)
        row_copy(r, 1).start(priority=1)
    for r in range(tm):
        row_copy(r, 0).wait()
        row_copy(r, 1).wait()
    wt = wt_ref[...]
    y = wt[:, 0:1] * _load_row_tiles(ybuf0, tm) + wt[:, 1:2] * _load_row_tiles(ybuf1, tm)
    o_ref[...] = x1_ref[...] + gt_ref[...] * y


def _combine(x1, mods, wt, dest, ys, *, sample, t_len, tm):
    n, d = x1.shape
    k = d // LANE
    row = lambda w: pl.BlockSpec((tm, w), lambda i: (i, 0))
    if sample:
        mod_spec = pl.BlockSpec((tm, d), lambda i: (i, 5))
    else:
        tpb = t_len // tm
        mod_spec = pl.BlockSpec((None, 1, d), lambda i: (i // tpb, 0, 5))
    return pl.pallas_call(
        functools.partial(_combine_kernel, tm, k),
        out_shape=jax.ShapeDtypeStruct((n, d), F32),
        grid=(n // tm,),
        in_specs=[row(d), mod_spec, row(LANE), pl.BlockSpec(memory_space=pl.ANY),
                  pl.BlockSpec(memory_space=pl.ANY)],
        out_specs=row(d),
        scratch_shapes=[pltpu.SMEM((2 * tm,), jnp.int32), pltpu.VMEM((tm * k, LANE), F32),
                        pltpu.VMEM((tm * k, LANE), F32), pltpu.SemaphoreType.DMA, pltpu.SemaphoreType.DMA],
        compiler_params=_cparams(("arbitrary",)),
        name="moe_combine_sample" if sample else "moe_combine_prompt",
    )(x1, mods, wt, dest, ys)


def _route_tables(counts, routes, tms, tm_e, n_tiles):
    n_exp = counts.shape[0]
    cnt = counts.astype(jnp.int32)
    padded = ((cnt + tm_e - 1) // tm_e) * tm_e
    ends = jnp.cumsum(padded)
    off = ends - padded
    n_used = (ends[-1] // tm_e).reshape(1)
    tile_start = jnp.arange(n_tiles, dtype=jnp.int32) * tm_e
    tile_expert = jnp.minimum(jnp.sum(tile_start[:, None] >= ends[None, :], axis=1), n_exp - 1).astype(jnp.int32)
    eids = jnp.arange(n_exp, dtype=jnp.int32)[:, None]
    dests = []
    for route, tm in zip(routes, tms):
        n = route.shape[1]
        e = route[0:2].astype(jnp.int32)
        base = jnp.sum(jnp.where(e[:, None, :] == eids[None], off[None, :, None], 0), axis=1)
        dest = base + route[2:4].astype(jnp.int32)
        dests.append(dest.reshape(2, n // tm, tm).transpose(1, 0, 2).reshape(n // tm, 2 * tm))
    return tile_expert, n_used, dests


def _rope_tables(pos, rope):
    half = rope // 2
    freqs = ROPE_THETA ** (-jnp.arange(half, dtype=F32) / half)
    ang = pos.astype(F32)[:, None] * freqs[None, :]
    cos, sin = jnp.cos(ang), jnp.sin(ang)
    t = pos.shape[0]
    pad = LANE - QK_NOPE - rope
    cq = jnp.concatenate([jnp.ones((t, QK_NOPE), F32), cos, cos, jnp.zeros((t, pad), F32)], axis=-1)
    sq = jnp.concatenate([jnp.zeros((t, QK_NOPE), F32), sin, sin, jnp.zeros((t, pad), F32)], axis=-1)
    return cq, sq


def _rot_cols(w, rope):
    half = rope // 2
    return jnp.concatenate([-w[..., half:], w[..., :half]], axis=-1)


def _prep_layer(l, w_in, w_uq, w_ukv, g_qk_q, g_qk_k, rope, gw, qrank, kvrank, vhead):
    d = w_in.shape[1]
    qk_head = QK_NOPE + rope
    pad = LANE - qk_head
    o = 2 * gw + qrank + kvrank
    w_kr = w_in[l][:, o:o + rope]
    z64 = jnp.zeros((d, QK_NOPE), F32)
    zp = jnp.zeros((d, pad), F32)
    w_in_p = jnp.concatenate([w_in[l][:, :o], z64, w_kr, zp, z64, _rot_cols(w_kr, rope), zp], axis=-1).astype(BF16)
    wq = w_uq[l].reshape(qrank, M_HEADS, qk_head)
    zq = jnp.zeros((qrank, M_HEADS, pad), F32)
    wq_plain = jnp.concatenate([wq, zq], axis=-1).reshape(qrank, M_HEADS * LANE)
    wq_rot = jnp.concatenate([jnp.zeros((qrank, M_HEADS, QK_NOPE), F32), _rot_cols(wq[..., QK_NOPE:], rope), zq],
                             axis=-1).reshape(qrank, M_HEADS * LANE)
    w_uq_p = jnp.concatenate([wq_plain, wq_rot], axis=-1).astype(BF16)
    wkv = w_ukv[l].reshape(kvrank, M_HEADS, QK_NOPE + vhead)
    wk = wkv[..., :QK_NOPE]
    wk_pad = jnp.concatenate([wk, jnp.zeros((kvrank, M_HEADS, LANE - QK_NOPE), F32)], axis=-1)
    wv = wkv[..., QK_NOPE:].reshape(kvrank, M_HEADS * vhead)
    w_kv_p = jnp.concatenate([wk_pad.reshape(kvrank, M_HEADS * LANE), wv], axis=-1).astype(BF16)
    wukt_pad = jnp.concatenate([wk.transpose(1, 2, 0), jnp.zeros((M_HEADS, LANE - QK_NOPE, kvrank), F32)],
                               axis=1).astype(BF16)
    wukt = wk.transpose(1, 2, 0).reshape(M_HEADS * QK_NOPE, kvrank).astype(BF16)
    zg = jnp.zeros((pad,), F32)
    gq_p = (jnp.concatenate([g_qk_q[l], zg]) * (qk_head ** -0.5)).reshape(1, LANE)
    gk_p = jnp.concatenate([g_qk_k[l], zg]).reshape(1, LANE)
    return w_in_p, w_uq_p, w_kv_p, wukt_pad, wukt, wv.astype(BF16), gq_p, gk_p


def kernel(x_prompt, x_sample, c_prompt, c_sample, cache_kv_latent, cache_k_rope, page_table, w_ada, b_ada,
           g_norm_mix, g_norm_ffn, w_in, g_v, w_s, b_s, g_cq, w_uq, g_ckv, w_ukv, g_qk_q, g_qk_k, g_mix_out,
           w_out, w_router, b_router, w_gate, w_up, w_down):
    batch, t_p, d = x_prompt.shape
    s_n, t_s, _ = x_sample.shape
    depth = w_ada.shape[0]
    gheads, chunk = w_s.shape[1], w_s.shape[2]
    hd = g_v.shape[-1]
    gw = gheads * hd
    qrank = g_cq.shape[-1]
    kvrank = g_ckv.shape[-1]
    rope = cache_k_rope.shape[-1]
    vhead = w_ukv.shape[-1] // M_HEADS - QK_NOPE
    past = page_table.shape[1] * cache_kv_latent.shape[2]
    assert t_s == 1 and t_p % chunk == 0 and g_qk_q.shape[-1] == QK_NOPE + rope
    row2 = lambda a: a.reshape(1, -1)

    mods = _ada(jnp.concatenate([c_prompt, c_sample], axis=0), w_ada, b_ada)
    tab_p = _rope_tables(jnp.arange(t_p, dtype=jnp.int32), rope)
    tab_s = _rope_tables(past + jnp.arange(t_s, dtype=jnp.int32), rope)
    w_router_t = w_router.T
    b_router_c = b_router.reshape(-1, 1)

    tm_p = 512 if t_p % 512 == 0 else 256
    tm_d = 256
    tm_s = s_n
    n_exp = w_router.shape[1]
    tm_e = 512
    n_pairs = 2 * (batch * t_p + s_n)
    n_tiles_e = -(-n_pairs // tm_e) + n_exp
    tq = 512 if t_p % 512 == 0 else 256
    cache_krt = jnp.swapaxes(cache_k_rope, 2, 3)
    xp = x_prompt.reshape(batch * t_p, d)
    xs = x_sample.reshape(s_n, d)
    open_p = ((t_p - 1) // chunk) * chunk
    outs = [[] for _ in range(6)]
    for l in range(depth):
        w_in_p, w_uq_p, w_kv_p, wukt_pad, wukt, wuv, gq_p, gk_p = _prep_layer(
            l, w_in, w_uq, w_ukv, g_qk_q, g_qk_k, rope, gw, qrank, kvrank, vhead)
        g_mix_g, g_mix_a = row2(g_mix_out[l, :gw]), row2(g_mix_out[l, gw:])
        w_out_b = w_out[l].astype(BF16)
        base = (row2(g_norm_mix[l]), w_in_p, row2(g_v[l]), row2(g_cq[l]), w_uq_p, row2(g_ckv[l]), w_kv_p,
                gq_p, gk_p, g_mix_g)
        mp = mods[l, :batch].reshape(batch, 1, N_MOD * d)
        ms = mods[l, batch:]

        w00 = row2(jnp.repeat(w_s[l, :, 0, 0], hd))
        b0 = row2(jnp.repeat(b_s[l, :, 0], hd))
        gn, v, q, ckv, kr, k, qa = _mixer_in(xs, ms, base + ((w00, b0, wukt_pad),), tab_s,
                                             sample=True, t_len=t_s, tm=tm_s, rope=rope)
        attn = _sample_attn(page_table, q, qa, k, ckv, gk_p, wukt, wuv, cache_kv_latent, cache_krt, layer=l)
        x1s, h2s, route_s, wt_s, cnt = _mixer_out(gn, attn, xs, ms, g_mix_a, w_out_b, row2(g_norm_ffn[l]),
                                                  w_router_t, b_router_c, jnp.zeros((n_exp, LANE), F32),
                                                  sample=True, t_len=t_s, tm=tm_s)
        outs[3].append(ckv.reshape(s_n, t_s, kvrank))
        outs[4].append(kr.reshape(s_n, t_s, rope))
        outs[5].append(v.reshape(s_n, t_s, gheads, hd))

        bs_full = jnp.broadcast_to(b_s[l][:, :, None], (gheads, chunk, hd))
        gn, v, q, ckv, kr, k, vv = _mixer_in(xp, mp, base + ((w_s[l], bs_full),), tab_p,
                                             sample=False, t_len=t_p, tm=tm_p, rope=rope)
        attn = _prompt_attn(q, k, vv, batch=batch, t_len=t_p, tq=tq)
        x1p, h2p, route_p, wt_p, cnt = _mixer_out(gn, attn, xp, mp, g_mix_a, w_out_b, row2(g_norm_ffn[l]),
                                                  w_router_t, b_router_c, cnt, sample=False, t_len=t_p, tm=tm_p)
        outs[0].append(ckv.reshape(batch, t_p, kvrank))
        outs[1].append(kr.reshape(batch, t_p, rope))
        outs[2].append(v.reshape(batch, t_p, gw)[:, open_p:].reshape(batch, t_p - open_p, gheads, hd))

        tile_expert, n_used, (dest_s, dest_p) = _route_tables(
            cnt[:, 0], (route_s, route_p), (tm_s, tm_d), tm_e, n_tiles_e)
        xsort = jnp.zeros((n_tiles_e * tm_e * (d // LANE), LANE), F32)
        xsort = _dispatch(h2s, dest_s, xsort, tm=tm_s, k=d // LANE)
        xsort = _dispatch(h2p, dest_p, xsort, tm=tm_d, k=d // LANE)
        ysort = _experts(tile_expert, n_used, xsort, w_gate, w_up, w_down, layer=l, tm=tm_e)
        xs = _combine(x1s, ms, wt_s, dest_s, ysort, sample=True, t_len=t_s, tm=tm_s)
        xp = _combine(x1p, mp, wt_p, dest_p, ysort, sample=False, t_len=t_p, tm=tm_d)

    return (xp.reshape(batch, t_p, d), xs.reshape(s_n, t_s, d), jnp.stack(outs[0]), jnp.stack(outs[1]),
            jnp.stack(outs[2]), jnp.stack(outs[3]), jnp.stack(outs[4]), jnp.stack(outs[5]))
```

```python
import functools

import jax
import jax.numpy as jnp
from jax import lax
from jax.experimental import pallas as pl
from jax.experimental.pallas import tpu as pltpu

F32 = jnp.float32
BF16 = jnp.bfloat16

M_HEADS = 8
QK_NOPE = 64
N_GROUPS = 4
N_MOD = 6
ROPE_THETA = 10000.0
EPS = 1e-6
LANE = 128
MXU_TILE = 256
VMEM_LIMIT = 56 * 1024 * 1024


def _cparams(sem):
    return pltpu.CompilerParams(dimension_semantics=sem, vmem_limit_bytes=VMEM_LIMIT)


def _dot(a, b):
    return jnp.dot(a, b, preferred_element_type=F32)


def _dot_nt(a, b, precision=None):
    return lax.dot_general(a, b, (((1,), (1,)), ((), ())), preferred_element_type=F32,
                           precision=precision)


def _load_row_tiles(ref, rows):
    k = ref.shape[0] // rows
    return jnp.concatenate([ref[pl.ds(c, rows, stride=k), :] for c in range(k)], axis=-1)


def _store_row_tiles(ref, x):
    rows = x.shape[0]
    k = ref.shape[0] // rows
    for c in range(k):
        ref[pl.ds(c, rows, stride=k), :] = x[:, c * LANE:(c + 1) * LANE]


def _rms(x, g, n=None):
    n = x.shape[-1] if n is None else n
    ms = jnp.sum(x * x, axis=-1, keepdims=True) * (1.0 / n)
    return x * lax.rsqrt(ms + EPS) * g


def _ada_kernel(c_ref, w_ref, b_ref, o_ref):
    c = c_ref[...]
    s = c * jax.nn.sigmoid(c)
    o_ref[0] = _dot(s.astype(BF16), w_ref[0].astype(BF16)) + b_ref[0]


def _ada(c_all, w_ada, b_ada):
    depth, d, n6 = w_ada.shape
    rows = c_all.shape[0]
    tn = 1536
    return pl.pallas_call(
        _ada_kernel,
        out_shape=jax.ShapeDtypeStruct((depth, rows, n6), F32),
        grid=(depth, n6 // tn),
        in_specs=[pl.BlockSpec((rows, d), lambda l, j: (0, 0)),
                  pl.BlockSpec((1, d, tn), lambda l, j: (l, 0, j)),
                  pl.BlockSpec((1, 1, tn), lambda l, j: (l, 0, j))],
        out_specs=pl.BlockSpec((1, rows, tn), lambda l, j: (l, 0, j)),
        compiler_params=_cparams(("arbitrary", "arbitrary")),
        name="ada_modulation",
    )(c_all, w_ada, b_ada.reshape(depth, 1, n6))


def _mixer_in_kernel(dims, sample, *refs):
    gw, qrank, kvrank, gheads, vw = dims
    (x_ref, sh_ref, sc_ref, gnm_ref, win_ref, gv_ref, gcq_ref, wuq_ref, gckv_ref, wkv_ref,
     gq_ref, gk_ref, cq_ref, sq_ref, gmix_ref) = refs[:15]
    if sample:
        w00_ref, b0_ref, wukt_ref = refs[15:18]
        gn_ref, v_ref, q_ref, ckv_ref, kr_ref, k_ref, qa_ref = refs[18:]
    else:
        ws_ref, bs_ref = refs[15:17]
        gn_ref, v_ref, q_ref, ckv_ref, kr_ref, k_ref, vv_ref = refs[17:]

    x = x_ref[...]
    h = _rms(x, gnm_ref[...]) * (1.0 + sc_ref[...]) + sh_ref[...]
    z = _dot(h.astype(BF16), win_ref[...])
    tm = z.shape[0]
    u = jax.nn.gelu(z[:, :gw])
    vg = jax.nn.gelu(z[:, gw:2 * gw])
    hd = gw // gheads
    gv = gv_ref[...]
    v = jnp.concatenate([_rms(vg[:, i * hd:(i + 1) * hd], gv) for i in range(gheads)], axis=-1)
    v_ref[...] = v

    if sample:
        s = v * w00_ref[...] + b0_ref[...]
    else:
        chunk = ws_ref.shape[-1]
        row = lax.broadcasted_iota(jnp.int32, (chunk, chunk), 0)
        col = lax.broadcasted_iota(jnp.int32, (chunk, chunk), 1)
        vb = v.astype(BF16)
        cols = []
        for i in range(gheads):
            wt = jnp.where(col <= row, ws_ref[i], 0.0).astype(BF16)
            rows = [_dot(wt, vb[c * chunk:(c + 1) * chunk, i * hd:(i + 1) * hd]) + bs_ref[i]
                    for c in range(tm // chunk)]
            cols.append(jnp.concatenate(rows, axis=0) if len(rows) > 1 else rows[0])
        s = jnp.concatenate(cols, axis=-1)
    g = u * s
    gn_ref[...] = _rms(g, gmix_ref[...]).astype(gn_ref.dtype)

    o0 = 2 * gw
    cq = _rms(z[:, o0:o0 + qrank], gcq_ref[...])
    qq = _dot(cq.astype(BF16), wuq_ref[...])
    hw = M_HEADS * LANE
    cqt = cq_ref[...]
    sqt = sq_ref[...]
    gq = gq_ref[...]
    n_real = QK_NOPE + kr_ref.shape[-1]
    q_heads = []
    for i in range(M_HEADS):
        qh = qq[:, i * LANE:(i + 1) * LANE] * cqt + qq[:, hw + i * LANE:hw + (i + 1) * LANE] * sqt
        q_heads.append(_rms(qh, gq, n_real))
    q = jnp.concatenate(q_heads, axis=-1)
    q_ref[...] = q.astype(q_ref.dtype)

    o1 = o0 + qrank
    ckv = _rms(z[:, o1:o1 + kvrank], gckv_ref[...])
    ckv_ref[...] = ckv
    o2 = o1 + kvrank
    krp = z[:, o2:o2 + LANE] * cqt + z[:, o2 + LANE:o2 + 2 * LANE] * sqt
    rope = kr_ref.shape[-1]
    kr_ref[...] = krp[:, QK_NOPE:QK_NOPE + rope]
    kv = _dot(ckv.astype(BF16), wkv_ref[...])
    kr_ss = jnp.sum(krp * krp, axis=-1, keepdims=True)
    gk = gk_ref[...]
    k_heads = []
    for i in range(M_HEADS):
        kn = kv[:, i * LANE:(i + 1) * LANE]
        ss = jnp.sum(kn * kn, axis=-1, keepdims=True) + kr_ss
        k_heads.append((kn + krp) * lax.rsqrt(ss * (1.0 / n_real) + EPS) * gk)
    k_ref[...] = jnp.concatenate(k_heads, axis=-1).astype(k_ref.dtype)
    if sample:
        lane = lax.broadcasted_iota(jnp.int32, (1, LANE), 1)
        gk_nope = jnp.where(lane < QK_NOPE, gk, 0.0)
        qa = [_dot((q_heads[i] * gk_nope).astype(BF16), wukt_ref[i]) for i in range(M_HEADS)]
        qa_ref[...] = jnp.concatenate(qa, axis=-1)
    else:
        vv_ref[...] = kv[:, hw:hw + vw].astype(vv_ref.dtype)


def _mixer_in(x, mods, lw, tabs, *, sample, t_len, tm, rope):
    n, d = x.shape
    (g_norm_mix, w_in_p, g_v, g_cq, w_uq_p, g_ckv, w_kv_p, gq_p, gk_p, g_mix_g, extra) = lw
    cq_tab, sq_tab = tabs
    gw = g_mix_g.shape[-1]
    qrank = g_cq.shape[-1]
    kvrank = g_ckv.shape[-1]
    gheads = gw // g_v.shape[-1]
    assert w_in_p.shape[1] == 2 * gw + qrank + kvrank + 2 * LANE
    vw = w_kv_p.shape[1] - M_HEADS * LANE
    nt = n // tm
    full = lambda a: pl.BlockSpec(a.shape, lambda i: (0,) * a.ndim)
    if sample:
        mod_spec = lambda k: pl.BlockSpec((tm, d), lambda i, k=k: (i, k))
        tab_spec = pl.BlockSpec((1, LANE), lambda i: (0, 0))
    else:
        tpb = t_len // tm
        mod_spec = lambda k: pl.BlockSpec((None, 1, d), lambda i, k=k: (i // tpb, 0, k))
        tab_spec = pl.BlockSpec((tm, LANE), lambda i: (i % tpb, 0))
    row = lambda w: pl.BlockSpec((tm, w), lambda i: (i, 0))
    in_specs = [row(d), mod_spec(0), mod_spec(1), full(g_norm_mix), full(w_in_p), full(g_v), full(g_cq),
                full(w_uq_p), full(g_ckv), full(w_kv_p), full(gq_p), full(gk_p), tab_spec, tab_spec,
                full(g_mix_g)] + [full(a) for a in extra]
    kr_w = rope
    hw = M_HEADS * LANE
    if sample:
        out_shape = [jax.ShapeDtypeStruct((n, gw), BF16), jax.ShapeDtypeStruct((n, gw), F32),
                     jax.ShapeDtypeStruct((n, hw), F32), jax.ShapeDtypeStruct((n, kvrank), F32),
                     jax.ShapeDtypeStruct((n, kr_w), F32), jax.ShapeDtypeStruct((n, hw), F32),
                     jax.ShapeDtypeStruct((n, M_HEADS * kvrank), F32)]
        out_specs = [row(gw), row(gw), row(hw), row(kvrank), row(kr_w), row(hw), row(M_HEADS * kvrank)]
    else:
        out_shape = [jax.ShapeDtypeStruct((n, gw), BF16), jax.ShapeDtypeStruct((n, gw), F32),
                     jax.ShapeDtypeStruct((n, hw), BF16), jax.ShapeDtypeStruct((n, kvrank), F32),
                     jax.ShapeDtypeStruct((n, kr_w), F32), jax.ShapeDtypeStruct((n, hw), BF16),
                     jax.ShapeDtypeStruct((n, vw), BF16)]
        out_specs = [row(gw), row(gw), row(hw), row(kvrank), row(kr_w), row(hw), row(vw)]
    dims = (gw, qrank, kvrank, gheads, vw)
    return pl.pallas_call(
        functools.partial(_mixer_in_kernel, dims, sample),
        out_shape=out_shape,
        grid=(nt,),
        in_specs=in_specs,
        out_specs=out_specs,
        compiler_params=_cparams(("arbitrary",)),
        name="mixer_in_sample" if sample else "mixer_in_prompt",
    )(x, mods, mods, g_norm_mix, w_in_p, g_v, g_cq, w_uq_p, g_ckv, w_kv_p, gq_p, gk_p, cq_tab, sq_tab,
      g_mix_g, *extra)


def _prompt_attn_kernel(tq, vhead, q_ref, k_ref, v_ref, o_ref):
    qi = pl.program_id(2)
    heads = q_ref.shape[-1] // LANE
    qs = [q_ref[:, h * LANE:(h + 1) * LANE] for h in range(heads)]

    def scores(j, h):
        start = pl.multiple_of(j * tq, tq)
        k = k_ref[pl.ds(start, tq), h * LANE:(h + 1) * LANE]
        v = v_ref[pl.ds(start, tq), h * vhead:(h + 1) * vhead]
        return _dot_nt(qs[h], k), v

    def update(carry, s, v):
        m, l, acc = carry
        m_new = jnp.maximum(m, jnp.max(s, axis=-1, keepdims=True))
        alpha = jnp.exp(m - m_new)
        p = jnp.exp(s - m_new)
        l = l * alpha + jnp.sum(p, axis=-1, keepdims=True)
        acc = acc * alpha + _dot(p.astype(BF16), v)
        return m_new, l, acc

    def body(j, carry):
        return tuple(update(carry[h], *scores(j, h)) for h in range(heads))

    init = tuple((jnp.full((tq, 1), -jnp.inf, F32), jnp.zeros((tq, 1), F32), jnp.zeros((tq, vhead), F32))
                 for _ in range(heads))
    carry = lax.fori_loop(0, qi, body, init)
    causal = (lax.broadcasted_iota(jnp.int32, (tq, tq), 1) <= lax.broadcasted_iota(jnp.int32, (tq, tq), 0))
    outs = []
    for h in range(heads):
        s, v = scores(qi, h)
        m, l, acc = update(carry[h], jnp.where(causal, s, -jnp.inf), v)
        outs.append(acc / l)
    o_ref[...] = jnp.concatenate(outs, axis=-1).astype(o_ref.dtype)


def _prompt_attn(q, k, v, *, batch, t_len, tq):
    n = q.shape[0]
    vhead = v.shape[1] // M_HEADS
    hp = LANE // vhead
    nq = t_len // tq
    return pl.pallas_call(
        functools.partial(_prompt_attn_kernel, tq, vhead),
        out_shape=jax.ShapeDtypeStruct((n, v.shape[1]), F32),
        grid=(batch, M_HEADS // hp, nq),
        in_specs=[pl.BlockSpec((tq, hp * LANE), lambda b, h, i: (b * nq + i, h)),
                  pl.BlockSpec((t_len, hp * LANE), lambda b, h, i: (b, h)),
                  pl.BlockSpec((t_len, hp * vhead), lambda b, h, i: (b, h))],
        out_specs=pl.BlockSpec((tq, hp * vhead), lambda b, h, i: (b * nq + i, h)),
        compiler_params=_cparams(("arbitrary", "arbitrary", "arbitrary")),
        name="prompt_attention",
    )(q, k, v)


def _sample_attn_kernel(layer, n_pages, ppt, tpi, pt_ref, q_ref, qa_ref, knew_ref, cnew_ref, gk_ref, wukt_ref,
                        wuv_ref, ckv_hbm, krt_hbm, o_ref, cbuf, rbuf, sem_c, sem_r, wext_ref, s_scr, p_scr,
                        kt_a, kt_b):
    seq = pl.program_id(0)
    n_seq = pl.num_programs(0)
    slot = lax.rem(seq, 2)
    nxt_slot = 1 - slot
    nxt = jnp.minimum(seq + 1, n_seq - 1)
    rope, page = rbuf.shape[2], rbuf.shape[3]
    tile = ppt * page
    n_tiles = n_pages // ppt
    hn = wukt_ref.shape[0]
    n_real = QK_NOPE + rope

    def page_copies(sq, sl, i):
        pg = pt_ref[sq * n_pages + i]
        return (pltpu.make_async_copy(ckv_hbm.at[layer, pg], cbuf.at[sl, pl.ds(i * page, page)], sem_c.at[sl]),
                pltpu.make_async_copy(krt_hbm.at[layer, pg], rbuf.at[sl, i], sem_r.at[sl]))

    def start_page(sq, sl, i):
        for cp in page_copies(sq, sl, i):
            cp.start()

    def wait_pages(sq, sl):
        def body(i, carry):
            for cp in page_copies(sq, sl, i):
                cp.wait()
            return carry
        lax.fori_loop(0, n_pages, body, 0)

    @pl.when(seq == 0)
    def _():
        def body(i, carry):
            start_page(0, 0, i)
            return carry
        lax.fori_loop(0, n_pages, body, 0)

    wait_pages(seq, slot)

    q = q_ref[0]
    pad = wext_ref.shape[0] - hn - M_HEADS
    wext_ref[...] = jnp.concatenate(
        [wukt_ref[...], qa_ref[0].astype(BF16), jnp.zeros((pad, wext_ref.shape[1]), BF16)], axis=0)
    qr = (q * gk_ref[...])[:, QK_NOPE:QK_NOPE + rope].astype(BF16)

    def c_tile(t):
        return cbuf[slot, pl.ds(pl.multiple_of(t * tile, tile), tile), :].astype(BF16)

    def expand(g, kt_ref):
        for u in range(tpi):
            kt_ref[u] = _dot_nt(wext_ref[...], c_tile(g * tpi + u))

    def score(g, kt_ref):
        for u in range(tpi):
            t = g * tpi + u
            for k in range(ppt):
                start_page(nxt, nxt_slot, t * ppt + k)
            krt = jnp.concatenate([rbuf[slot, t * ppt + k] for k in range(ppt)], axis=1)
            kn = kt_ref[u, :hn, :]
            ss = jnp.sum((kn * kn).reshape(M_HEADS, QK_NOPE, tile), axis=1)
            kr_ss = jnp.sum(krt * krt, axis=0, keepdims=True)
            rinv = lax.rsqrt((ss + kr_ss) * (1.0 / n_real) + EPS)
            s_scr[t] = (kt_ref[u, hn:hn + M_HEADS, :] + _dot(qr, krt.astype(BF16))) * rinv

    n_groups = n_tiles // tpi
    n_pairs = (n_groups - 1) // 2
    expand(0, kt_a)

    def pair(h, carry):
        g = 2 * h
        expand(g + 1, kt_b)
        score(g, kt_a)
        expand(g + 2, kt_a)
        score(g + 1, kt_b)
        return carry

    lax.fori_loop(0, n_pairs, pair, 0)
    g_tail = 2 * n_pairs
    if n_groups - g_tail == 2:
        expand(g_tail + 1, kt_b)
        score(g_tail, kt_a)
        score(g_tail + 1, kt_b)
    else:
        score(g_tail, kt_a)

    s_all = s_scr[...]
    s_new = jnp.sum(q * knew_ref[0], axis=-1, keepdims=True)
    m = jnp.maximum(jnp.max(jnp.max(s_all, axis=0), axis=-1, keepdims=True), s_new)
    p_all = jnp.exp(s_all - m)
    p_new = jnp.exp(s_new - m)
    l = jnp.sum(jnp.sum(p_all, axis=0), axis=-1, keepdims=True) + p_new
    p_scr[...] = p_all

    def phase_c(g, accs):
        return tuple(accs[u] + _dot(p_scr[g * tpi + u].astype(BF16), c_tile(g * tpi + u)) for u in range(tpi))

    zero = jnp.zeros((M_HEADS, cbuf.shape[-1]), F32)
    accs = lax.fori_loop(0, n_tiles // tpi, phase_c, (zero,) * tpi)
    acc = functools.reduce(lambda a, b: a + b, accs)
    c_new = cnew_ref[0].astype(BF16).astype(F32)
    o_lat = (acc + p_new.astype(BF16).astype(F32) * c_new) / l
    full = _dot(o_lat.astype(BF16), wuv_ref[...])
    vhead = full.shape[-1] // M_HEADS
    hrow = lax.broadcasted_iota(jnp.int32, full.shape, 0)
    hcol = lax.broadcasted_iota(jnp.int32, full.shape, 1) // vhead
    o_ref[0] = jnp.sum(jnp.where(hrow == hcol, full, 0.0), axis=0, keepdims=True)

    @pl.when(seq == n_seq - 1)
    def _():
        wait_pages(nxt, nxt_slot)


def _sample_attn(page_table, q, qa, k_new, c_new, gk_p, wukt, wuv, cache_kv, cache_krt, *, layer):
    s_n = q.shape[0]
    n_pages = page_table.shape[1]
    page = cache_kv.shape[2]
    kvrank = cache_kv.shape[3]
    rope = cache_krt.shape[2]
    ppt = max(1, MXU_TILE // page)
    assert n_pages % ppt == 0
    n_tiles = n_pages // ppt
    tpi = 4
    while n_tiles % tpi:
        tpi //= 2
    tile = ppt * page
    vw = wuv.shape[1]
    seq3 = lambda a, b: pl.BlockSpec((1, a, b), lambda s, pt: (s, 0, 0))
    full = lambda a: pl.BlockSpec(a.shape, lambda s, pt: (0,) * a.ndim)
    hbm = pl.BlockSpec(memory_space=pl.ANY)
    in_specs = [seq3(M_HEADS, LANE), seq3(M_HEADS, kvrank), seq3(M_HEADS, LANE), seq3(1, kvrank),
                full(gk_p), full(wukt), full(wuv), hbm, hbm]
    wext_rows = wukt.shape[0] + 2 * M_HEADS
    return pl.pallas_call(
        functools.partial(_sample_attn_kernel, layer, n_pages, ppt, tpi),
        out_shape=jax.ShapeDtypeStruct((s_n, 1, vw), F32),
        grid_spec=pltpu.PrefetchScalarGridSpec(
            num_scalar_prefetch=1, grid=(s_n,), in_specs=in_specs,
            out_specs=pl.BlockSpec((1, 1, vw), lambda s, pt: (s, 0, 0)),
            scratch_shapes=[pltpu.VMEM((2, n_pages * page, kvrank), F32),
                            pltpu.VMEM((2, n_pages, rope, page), F32),
                            pltpu.SemaphoreType.DMA((2,)), pltpu.SemaphoreType.DMA((2,)),
                            pltpu.VMEM((wext_rows, kvrank), BF16),
                            pltpu.VMEM((n_tiles, M_HEADS, tile), F32),
                            pltpu.VMEM((n_tiles, M_HEADS, tile), F32),
                            pltpu.VMEM((tpi, wext_rows, tile), F32),
                            pltpu.VMEM((tpi, wext_rows, tile), F32)]),
        compiler_params=_cparams(("arbitrary",)),
        name="sample_attention",
    )(page_table.reshape(-1), q.reshape(s_n, M_HEADS, LANE), qa.reshape(s_n, M_HEADS, kvrank),
      k_new.reshape(s_n, M_HEADS, LANE), c_new.reshape(s_n, 1, kvrank), gk_p, wukt, wuv,
      cache_kv, cache_krt).reshape(s_n, vw)


def _mixer_out_kernel(n_exp, gn_ref, a_ref, x_ref, gt_ref, sh_ref, sc_ref, gmix_ref, wout_ref, gffn_ref,
                      wr_ref, br_ref, cnt_in_ref, x1_ref, h2_ref, route_ref, wt_ref, cnt_ref, run_ref):
    @pl.when(pl.program_id(0) == 0)
    def _():
        run_ref[...] = cnt_in_ref[...]

    gw = gn_ref.shape[-1]
    an = _rms(a_ref[...].astype(F32), gmix_ref[...])
    y = _dot(gn_ref[...], wout_ref[:gw, :]) + _dot(an.astype(BF16), wout_ref[gw:, :])
    x1 = x_ref[...] + gt_ref[...] * y
    x1_ref[...] = x1
    h2 = _rms(x1, gffn_ref[...]) * (1.0 + sc_ref[...]) + sh_ref[...]
    _store_row_tiles(h2_ref, h2)
    tm = h2.shape[0]

    logits = _dot_nt(wr_ref[...], h2, precision=lax.Precision.HIGHEST)
    scores = jax.nn.sigmoid(logits)
    sel = scores + br_ref[...]
    per = n_exp // N_GROUPS
    best = None
    for g in range(N_GROUPS):
        a, b, c, d = [sel[g * per + i:g * per + i + 1, :] for i in range(per)]
        hi1, lo1 = jnp.maximum(a, b), jnp.minimum(a, b)
        hi2, lo2 = jnp.maximum(c, d), jnp.minimum(c, d)
        gs = jnp.maximum(hi1, hi2) + jnp.maximum(jnp.minimum(hi1, hi2), jnp.maximum(lo1, lo2))
        if best is None:
            best, grp = gs, jnp.zeros(gs.shape, jnp.int32)
        else:
            better = gs > best
            grp = jnp.where(better, g, grp)
            best = jnp.where(better, gs, best)
    erow = lax.broadcasted_iota(jnp.int32, (n_exp, tm), 0)
    selm = jnp.where(erow // per == grp, sel, -jnp.inf)
    m1 = jnp.max(selm, axis=0, keepdims=True)
    i1 = jnp.min(jnp.where(selm == m1, erow, n_exp), axis=0, keepdims=True)
    oh1 = erow == i1
    selm2 = jnp.where(oh1, -jnp.inf, selm)
    m2 = jnp.max(selm2, axis=0, keepdims=True)
    i2 = jnp.min(jnp.where(selm2 == m2, erow, n_exp), axis=0, keepdims=True)
    oh2 = erow == i2
    s1 = jnp.sum(jnp.where(oh1, scores, 0.0), axis=0, keepdims=True)
    s2 = jnp.sum(jnp.where(oh2, scores, 0.0), axis=0, keepdims=True)
    tot = s1 + s2
    w1, w2 = s1 / tot, s2 / tot

    oh = jnp.where(oh1 | oh2, 1.0, 0.0)
    tri = (lax.broadcasted_iota(jnp.int32, (tm, tm), 0)
           <= lax.broadcasted_iota(jnp.int32, (tm, tm), 1)).astype(BF16)
    before = _dot(oh.astype(BF16), tri) - oh + run_ref[:, 0:1]
    r1 = jnp.sum(jnp.where(oh1, before, 0.0), axis=0, keepdims=True)
    r2 = jnp.sum(jnp.where(oh2, before, 0.0), axis=0, keepdims=True)
    run = run_ref[...] + jnp.sum(oh, axis=1, keepdims=True)
    run_ref[...] = run
    cnt_ref[...] = run
    zero = jnp.zeros((1, tm), F32)
    route_ref[...] = jnp.concatenate([i1.astype(F32), i2.astype(F32), r1, r2, w1, w2, zero, zero], axis=0)
    wt_ref[...] = jnp.concatenate([w1, w2, jnp.zeros((LANE - 2, tm), F32)], axis=0).T


def _mixer_out(gn, attn, x, mods, g_mix_a, w_out_b, g_norm_ffn, w_router_t, b_router, cnt_in, *, sample, t_len,
               tm):
    n, d = x.shape
    gw = gn.shape[1]
    n_exp = w_router_t.shape[0]
    full = lambda a: pl.BlockSpec(a.shape, lambda i: (0,) * a.ndim)
    row = lambda w: pl.BlockSpec((tm, w), lambda i: (i, 0))
    if sample:
        mod_spec = lambda k: pl.BlockSpec((tm, d), lambda i, k=k: (i, k))
    else:
        tpb = t_len // tm
        mod_spec = lambda k: pl.BlockSpec((None, 1, d), lambda i, k=k: (i // tpb, 0, k))
    return pl.pallas_call(
        functools.partial(_mixer_out_kernel, n_exp),
        out_shape=[jax.ShapeDtypeStruct((n, d), F32), jax.ShapeDtypeStruct((n * (d // LANE), LANE), F32),
                   jax.ShapeDtypeStruct((8, n), F32), jax.ShapeDtypeStruct((n, LANE), F32),
                   jax.ShapeDtypeStruct((n_exp, LANE), F32)],
        grid=(n // tm,),
        in_specs=[row(gw), row(attn.shape[1]), row(d), mod_spec(2), mod_spec(3), mod_spec(4), full(g_mix_a),
                  full(w_out_b), full(g_norm_ffn), full(w_router_t), full(b_router), full(cnt_in)],
        out_specs=[row(d), pl.BlockSpec((tm * (d // LANE), LANE), lambda i: (i, 0)),
                   pl.BlockSpec((8, tm), lambda i: (0, i)), row(LANE),
                   pl.BlockSpec((n_exp, LANE), lambda i: (0, 0))],
        scratch_shapes=[pltpu.VMEM((n_exp, LANE), F32)],
        compiler_params=_cparams(("arbitrary",)),
        name="mixer_out_sample" if sample else "mixer_out_prompt",
    )(gn, attn, x, mods, mods, mods, g_mix_a, w_out_b, g_norm_ffn, w_router_t, b_router, cnt_in)


def _dispatch_kernel(tm, k, h_hbm, dest_hbm, xs_in, xs_hbm, hbuf, dsm, sem_h, sem_d, sem_r):
    del xs_in
    i = pl.program_id(0)
    n = pl.num_programs(0)
    slot = lax.rem(i, 2)
    other = 1 - slot

    def tile_loads(t, sl):
        rows = pl.ds(pl.multiple_of(t * (tm * k), tm * k), tm * k)
        return (pltpu.make_async_copy(h_hbm.at[rows], hbuf.at[sl], sem_h.at[sl]),
                pltpu.make_async_copy(dest_hbm.at[t], dsm.at[sl], sem_d.at[sl]))

    def row_copy(sl, r, c):
        dst = pl.multiple_of(dsm[sl, c * tm + r] * k, k)
        return pltpu.make_async_copy(hbuf.at[sl, pl.ds(r * k, k)], xs_hbm.at[pl.ds(dst, k)], sem_r.at[sl])

    def wait_rows(sl):
        for r in range(tm):
            row_copy(sl, r, 0).wait()
            row_copy(sl, r, 1).wait()

    @pl.when(i == 0)
    def _():
        for cp in tile_loads(0, 0):
            cp.start()

    @pl.when(i > 0)
    def _():
        wait_rows(other)

    @pl.when(i + 1 < n)
    def _():
        for cp in tile_loads(i + 1, other):
            cp.start()

    for cp in tile_loads(i, slot):
        cp.wait()
    for r in range(tm):
        row_copy(slot, r, 0).start(priority=0)
        row_copy(slot, r, 1).start(priority=1)

    @pl.when(i == n - 1)
    def _():
        wait_rows(slot)


def _dispatch(h2t, dest, xs, *, tm, k):
    n = h2t.shape[0] // k
    hbm = pl.BlockSpec(memory_space=pl.ANY)
    return pl.pallas_call(
        functools.partial(_dispatch_kernel, tm, k),
        out_shape=jax.ShapeDtypeStruct(xs.shape, xs.dtype),
        grid=(n // tm,),
        in_specs=[hbm, hbm, hbm],
        out_specs=hbm,
        scratch_shapes=[pltpu.VMEM((2, tm * k, LANE), F32), pltpu.SMEM((2, 2 * tm), jnp.int32),
                        pltpu.SemaphoreType.DMA((2,)), pltpu.SemaphoreType.DMA((2,)),
                        pltpu.SemaphoreType.DMA((2,))],
        input_output_aliases={2: 0},
        compiler_params=_cparams(("arbitrary",)),
        name="moe_dispatch",
    )(h2t, dest, xs)


def _experts_kernel(tm, te_ref, nu_ref, x_ref, wg_ref, wu_ref, wd_ref, y_ref):
    j = pl.program_id(0)

    @pl.when(j < nu_ref[0])
    def _():
        x = _load_row_tiles(x_ref, tm).astype(BF16)
        a = _dot(x, wg_ref[...].astype(BF16))
        b = _dot(x, wu_ref[...].astype(BF16))
        act = (a * jax.nn.sigmoid(a)) * b
        _store_row_tiles(y_ref, _dot(act.astype(BF16), wd_ref[...].astype(BF16)))

    @pl.when(j >= nu_ref[0])
    def _():
        y_ref[...] = jnp.zeros(y_ref.shape, y_ref.dtype)


def _experts(tile_expert, n_used, xs, w_gate, w_up, w_down, *, layer, tm):
    d, f = w_gate.shape[-2:]
    k = d // LANE
    p = xs.shape[0] // k
    x_spec = pl.BlockSpec((tm * k, LANE), lambda j, te, nu: (jnp.minimum(j, nu[0] - 1), 0))
    w_spec = lambda a, b: pl.BlockSpec((None, None, a, b), lambda j, te, nu: (layer, te[j], 0, 0))
    return pl.pallas_call(
        functools.partial(_experts_kernel, tm),
        out_shape=jax.ShapeDtypeStruct(xs.shape, F32),
        grid_spec=pltpu.PrefetchScalarGridSpec(
            num_scalar_prefetch=2, grid=(p // tm,),
            in_specs=[x_spec, w_spec(d, f), w_spec(d, f), w_spec(f, d)],
            out_specs=pl.BlockSpec((tm * k, LANE), lambda j, te, nu: (j, 0))),
        compiler_params=_cparams(("arbitrary",)),
        name="moe_experts",
    )(tile_expert, n_used, xs, w_gate, w_up, w_down)


def _combine_kernel(tm, k, x1_ref, gt_ref, wt_ref, dest_hbm, ys_hbm, o_ref, dsm, ybuf, sem_d, sem_r):
    i = pl.program_id(0)
    n = pl.num_programs(0)
    slot = lax.rem(i, 2)
    other = 1 - slot

    def dest_load(t):
        ds = lax.rem(t, 3)
        return pltpu.make_async_copy(dest_hbm.at[t], dsm.at[ds], sem_d.at[ds])

    def row_copy(t, sl, r, c):
        src = pl.multiple_of(dsm[lax.rem(t, 3), c * tm + r] * k, k)
        return pltpu.make_async_copy(ys_hbm.at[pl.ds(src, k)], ybuf.at[sl, c, pl.ds(r * k, k)], sem_r.at[sl])

    def start_rows(t, sl):
        for r in range(tm):
            row_copy(t, sl, r, 0).start(priority=0)
            row_copy(t, sl, r, 1).start(priority=1)

    @pl.when(i == 0)
    def _():
        dest_load(0).start()

        @pl.when(n > 1)
        def _():
            dest_load(1).start()

        dest_load(0).wait()
        start_rows(0, 0)

    @pl.when(i + 2 < n)
    def _():
        dest_load(i + 2).start()

    @pl.when(i + 1 < n)
    def _():
        dest_load(i + 1).wait()
        start_rows(i + 1, other)

    for r in range(tm):
        row_copy(i, slot, r, 0).wait()
        row_copy(i, slot, r, 1).wait()
    wt = wt_ref[...]
    y = (wt[:, 0:1] * _load_row_tiles(ybuf.at[slot, 0], tm) + wt[:, 1:2] * _load_row_tiles(ybuf.at[slot, 1], tm))
    o_ref[...] = x1_ref[...] + gt_ref[...] * y


def _combine(x1, mods, wt, dest, ys, *, sample, t_len, tm):
    n, d = x1.shape
    k = d // LANE
    row = lambda w: pl.BlockSpec((tm, w), lambda i: (i, 0))
    if sample:
        mod_spec = pl.BlockSpec((tm, d), lambda i: (i, 5))
    else:
        tpb = t_len // tm
        mod_spec = pl.BlockSpec((None, 1, d), lambda i: (i // tpb, 0, 5))
    return pl.pallas_call(
        functools.partial(_combine_kernel, tm, k),
        out_shape=jax.ShapeDtypeStruct((n, d), F32),
        grid=(n // tm,),
        in_specs=[row(d), mod_spec, row(LANE), pl.BlockSpec(memory_space=pl.ANY),
                  pl.BlockSpec(memory_space=pl.ANY)],
        out_specs=row(d),
        scratch_shapes=[pltpu.SMEM((3, 2 * tm), jnp.int32), pltpu.VMEM((2, 2, tm * k, LANE), F32),
                        pltpu.SemaphoreType.DMA((3,)), pltpu.SemaphoreType.DMA((2,))],
        compiler_params=_cparams(("arbitrary",)),
        name="moe_combine_sample" if sample else "moe_combine_prompt",
    )(x1, mods, wt, dest, ys)


def _route_tables(counts, routes, tms, tm_e, n_tiles):
    n_exp = counts.shape[0]
    cnt = counts.astype(jnp.int32)
    padded = ((cnt + tm_e - 1) // tm_e) * tm_e
    ends = jnp.cumsum(padded)
    off = ends - padded
    n_used = (ends[-1] // tm_e).reshape(1)
    tile_start = jnp.arange(n_tiles, dtype=jnp.int32) * tm_e
    tile_expert = jnp.minimum(jnp.sum(tile_start[:, None] >= ends[None, :], axis=1), n_exp - 1).astype(jnp.int32)
    eids = jnp.arange(n_exp, dtype=jnp.int32)[:, None]
    dests = []
    for route, tm in zip(routes, tms):
        n = route.shape[1]
        e = route[0:2].astype(jnp.int32)
        base = jnp.sum(jnp.where(e[:, None, :] == eids[None], off[None, :, None], 0), axis=1)
        dest = base + route[2:4].astype(jnp.int32)
        dests.append(dest.reshape(2, n // tm, tm).transpose(1, 0, 2).reshape(n // tm, 2 * tm))
    return tile_expert, n_used, dests


def _rope_tables(pos, rope):
    half = rope // 2
    freqs = ROPE_THETA ** (-jnp.arange(half, dtype=F32) / half)
    ang = pos.astype(F32)[:, None] * freqs[None, :]
    cos, sin = jnp.cos(ang), jnp.sin(ang)
    t = pos.shape[0]
    pad = LANE - QK_NOPE - rope
    cq = jnp.concatenate([jnp.ones((t, QK_NOPE), F32), cos, cos, jnp.zeros((t, pad), F32)], axis=-1)
    sq = jnp.concatenate([jnp.zeros((t, QK_NOPE), F32), sin, sin, jnp.zeros((t, pad), F32)], axis=-1)
    return cq, sq


def _rot_cols(w, rope):
    half = rope // 2
    return jnp.concatenate([-w[..., half:], w[..., :half]], axis=-1)


def _prep_layer(l, w_in, w_uq, w_ukv, g_qk_q, g_qk_k, rope, gw, qrank, kvrank, vhead):
    d = w_in.shape[1]
    qk_head = QK_NOPE + rope
    pad = LANE - qk_head
    o = 2 * gw + qrank + kvrank
    w_kr = w_in[l][:, o:o + rope]
    z64 = jnp.zeros((d, QK_NOPE), F32)
    zp = jnp.zeros((d, pad), F32)
    w_in_p = jnp.concatenate([w_in[l][:, :o], z64, w_kr, zp, z64, _rot_cols(w_kr, rope), zp], axis=-1).astype(BF16)
    wq = w_uq[l].reshape(qrank, M_HEADS, qk_head)
    zq = jnp.zeros((qrank, M_HEADS, pad), F32)
    wq_plain = jnp.concatenate([wq, zq], axis=-1).reshape(qrank, M_HEADS * LANE)
    wq_rot = jnp.concatenate([jnp.zeros((qrank, M_HEADS, QK_NOPE), F32), _rot_cols(wq[..., QK_NOPE:], rope), zq],
                             axis=-1).reshape(qrank, M_HEADS * LANE)
    w_uq_p = jnp.concatenate([wq_plain, wq_rot], axis=-1).astype(BF16)
    wkv = w_ukv[l].reshape(kvrank, M_HEADS, QK_NOPE + vhead)
    wk = wkv[..., :QK_NOPE]
    wk_pad = jnp.concatenate([wk, jnp.zeros((kvrank, M_HEADS, LANE - QK_NOPE), F32)], axis=-1)
    wv = wkv[..., QK_NOPE:].reshape(kvrank, M_HEADS * vhead)
    w_kv_p = jnp.concatenate([wk_pad.reshape(kvrank, M_HEADS * LANE), wv], axis=-1).astype(BF16)
    wukt_pad = jnp.concatenate([wk.transpose(1, 2, 0), jnp.zeros((M_HEADS, LANE - QK_NOPE, kvrank), F32)],
                               axis=1).astype(BF16)
    wukt = wk.transpose(1, 2, 0).reshape(M_HEADS * QK_NOPE, kvrank).astype(BF16)
    zg = jnp.zeros((pad,), F32)
    gq_p = (jnp.concatenate([g_qk_q[l], zg]) * (qk_head ** -0.5)).reshape(1, LANE)
    gk_p = jnp.concatenate([g_qk_k[l], zg]).reshape(1, LANE)
    return w_in_p, w_uq_p, w_kv_p, wukt_pad, wukt, wv.astype(BF16), gq_p, gk_p


def kernel(x_prompt, x_sample, c_prompt, c_sample, cache_kv_latent, cache_k_rope, page_table, w_ada, b_ada,
           g_norm_mix, g_norm_ffn, w_in, g_v, w_s, b_s, g_cq, w_uq, g_ckv, w_ukv, g_qk_q, g_qk_k, g_mix_out,
           w_out, w_router, b_router, w_gate, w_up, w_down):
    batch, t_p, d = x_prompt.shape
    s_n, t_s, _ = x_sample.shape
    depth = w_ada.shape[0]
    gheads, chunk = w_s.shape[1], w_s.shape[2]
    hd = g_v.shape[-1]
    gw = gheads * hd
    qrank = g_cq.shape[-1]
    kvrank = g_ckv.shape[-1]
    rope = cache_k_rope.shape[-1]
    vhead = w_ukv.shape[-1] // M_HEADS - QK_NOPE
    past = page_table.shape[1] * cache_kv_latent.shape[2]
    assert t_s == 1 and t_p % chunk == 0 and g_qk_q.shape[-1] == QK_NOPE + rope
    row2 = lambda a: a.reshape(1, -1)

    mods = _ada(jnp.concatenate([c_prompt, c_sample], axis=0), w_ada, b_ada)
    tab_p = _rope_tables(jnp.arange(t_p, dtype=jnp.int32), rope)
    tab_s = _rope_tables(past + jnp.arange(t_s, dtype=jnp.int32), rope)
    w_router_t = w_router.T
    b_router_c = b_router.reshape(-1, 1)

    tm_p = 512 if t_p % 512 == 0 else 256
    tm_d = 256
    tm_s = s_n
    n_exp = w_router.shape[1]
    tm_e = 512
    n_pairs = 2 * (batch * t_p + s_n)
    n_tiles_e = -(-n_pairs // tm_e) + n_exp
    tq = 512 if t_p % 512 == 0 else 256
    cache_krt = jnp.swapaxes(cache_k_rope, 2, 3)
    xp = x_prompt.reshape(batch * t_p, d)
    xs = x_sample.reshape(s_n, d)
    open_p = ((t_p - 1) // chunk) * chunk
    outs = [[] for _ in range(6)]
    for l in range(depth):
        w_in_p, w_uq_p, w_kv_p, wukt_pad, wukt, wuv, gq_p, gk_p = _prep_layer(
            l, w_in, w_uq, w_ukv, g_qk_q, g_qk_k, rope, gw, qrank, kvrank, vhead)
        g_mix_g, g_mix_a = row2(g_mix_out[l, :gw]), row2(g_mix_out[l, gw:])
        w_out_b = w_out[l].astype(BF16)
        base = (row2(g_norm_mix[l]), w_in_p, row2(g_v[l]), row2(g_cq[l]), w_uq_p, row2(g_ckv[l]), w_kv_p,
                gq_p, gk_p, g_mix_g)
        mp = mods[l, :batch].reshape(batch, 1, N_MOD * d)
        ms = mods[l, batch:]

        w00 = row2(jnp.repeat(w_s[l, :, 0, 0], hd))
        b0 = row2(jnp.repeat(b_s[l, :, 0], hd))
        gn, v, q, ckv, kr, k, qa = _mixer_in(xs, ms, base + ((w00, b0, wukt_pad),), tab_s,
                                             sample=True, t_len=t_s, tm=tm_s, rope=rope)
        attn = _sample_attn(page_table, q, qa, k, ckv, gk_p, wukt, wuv, cache_kv_latent, cache_krt, layer=l)
        x1s, h2s, route_s, wt_s, cnt = _mixer_out(gn, attn, xs, ms, g_mix_a, w_out_b, row2(g_norm_ffn[l]),
                                                  w_router_t, b_router_c, jnp.zeros((n_exp, LANE), F32),
                                                  sample=True, t_len=t_s, tm=tm_s)
        outs[3].append(ckv.reshape(s_n, t_s, kvrank))
        outs[4].append(kr.reshape(s_n, t_s, rope))
        outs[5].append(v.reshape(s_n, t_s, gheads, hd))

        bs_full = jnp.broadcast_to(b_s[l][:, :, None], (gheads, chunk, hd))
        gn, v, q, ckv, kr, k, vv = _mixer_in(xp, mp, base + ((w_s[l], bs_full),), tab_p,
                                             sample=False, t_len=t_p, tm=tm_p, rope=rope)
        attn = _prompt_attn(q, k, vv, batch=batch, t_len=t_p, tq=tq)
        x1p, h2p, route_p, wt_p, cnt = _mixer_out(gn, attn, xp, mp, g_mix_a, w_out_b, row2(g_norm_ffn[l]),
                                                  w_router_t, b_router_c, cnt, sample=False, t_len=t_p, tm=tm_p)
        outs[0].append(ckv.reshape(batch, t_p, kvrank))
        outs[1].append(kr.reshape(batch, t_p, rope))
        outs[2].append(v.reshape(batch, t_p, gw)[:, open_p:].reshape(batch, t_p - open_p, gheads, hd))

        tile_expert, n_used, (dest_s, dest_p) = _route_tables(
            cnt[:, 0], (route_s, route_p), (tm_s, tm_d), tm_e, n_tiles_e)
        xsort = jnp.zeros((n_tiles_e * tm_e * (d // LANE), LANE), F32)
        xsort = _dispatch(h2s, dest_s, xsort, tm=tm_s, k=d // LANE)
        xsort = _dispatch(h2p, dest_p, xsort, tm=tm_d, k=d // LANE)
        ysort = _experts(tile_expert, n_used, xsort, w_gate, w_up, w_down, layer=l, tm=tm_e)
        xs = _combine(x1s, ms, wt_s, dest_s, ysort, sample=True, t_len=t_s, tm=tm_s)
        xp = _combine(x1p, mp, wt_p, dest_p, ysort, sample=False, t_len=t_p, tm=tm_d)

    return (xp.reshape(batch, t_p, d), xs.reshape(s_n, t_s, d), jnp.stack(outs[0]), jnp.stack(outs[1]),
            jnp.stack(outs[2]), jnp.stack(outs[3]), jnp.stack(outs[4]), jnp.stack(outs[5]))
```

```python
import functools

import jax
import jax.numpy as jnp
from jax import lax
from jax.experimental import pallas as pl
from jax.experimental.pallas import tpu as pltpu

F32 = jnp.float32
BF16 = jnp.bfloat16

M_HEADS = 8
QK_NOPE = 64
N_GROUPS = 4
N_MOD = 6
ROPE_THETA = 10000.0
EPS = 1e-6
LANE = 128
MXU_TILE = 256
VMEM_LIMIT = 56 * 1024 * 1024


def _cparams(sem):
    return pltpu.CompilerParams(dimension_semantics=sem, vmem_limit_bytes=VMEM_LIMIT)


def _dot(a, b):
    return jnp.dot(a, b, preferred_element_type=F32)


def _dot_nt(a, b, precision=None):
    return lax.dot_general(a, b, (((1,), (1,)), ((), ())), preferred_element_type=F32,
                           precision=precision)


def _load_row_tiles(ref, rows):
    k = ref.shape[0] // rows
    return jnp.concatenate([ref[pl.ds(c, rows, stride=k), :] for c in range(k)], axis=-1)


def _store_row_tiles(ref, x):
    rows = x.shape[0]
    k = ref.shape[0] // rows
    for c in range(k):
        ref[pl.ds(c, rows, stride=k), :] = x[:, c * LANE:(c + 1) * LANE]


def _rms(x, g, n=None):
    n = x.shape[-1] if n is None else n
    ms = jnp.sum(x * x, axis=-1, keepdims=True) * (1.0 / n)
    return x * lax.rsqrt(ms + EPS) * g


def _ada_kernel(c_ref, w_ref, b_ref, o_ref):
    c = c_ref[...]
    s = c * jax.nn.sigmoid(c)
    o_ref[0] = _dot(s.astype(BF16), w_ref[0].astype(BF16)) + b_ref[0]


def _ada(c_all, w_ada, b_ada):
    depth, d, n6 = w_ada.shape
    rows = c_all.shape[0]
    tn = 1536
    return pl.pallas_call(
        _ada_kernel,
        out_shape=jax.ShapeDtypeStruct((depth, rows, n6), F32),
        grid=(depth, n6 // tn),
        in_specs=[pl.BlockSpec((rows, d), lambda l, j: (0, 0)),
                  pl.BlockSpec((1, d, tn), lambda l, j: (l, 0, j)),
                  pl.BlockSpec((1, 1, tn), lambda l, j: (l, 0, j))],
        out_specs=pl.BlockSpec((1, rows, tn), lambda l, j: (l, 0, j)),
        compiler_params=_cparams(("arbitrary", "arbitrary")),
        name="ada_modulation",
    )(c_all, w_ada, b_ada.reshape(depth, 1, n6))


def _mixer_in_kernel(dims, sample, *refs):
    gw, qrank, kvrank, gheads, vw = dims
    (x_ref, sh_ref, sc_ref, gnm_ref, win_ref, gv_ref, gcq_ref, wuq_ref, gckv_ref, wkv_ref,
     gq_ref, gk_ref, cq_ref, sq_ref, gmix_ref) = refs[:15]
    if sample:
        w00_ref, b0_ref, wukt_ref = refs[15:18]
        gn_ref, v_ref, q_ref, ckv_ref, kr_ref, k_ref, qa_ref = refs[18:]
    else:
        ws_ref, bs_ref = refs[15:17]
        gn_ref, v_ref, q_ref, ckv_ref, kr_ref, k_ref, vv_ref = refs[17:]

    x = x_ref[...]
    h = _rms(x, gnm_ref[...]) * (1.0 + sc_ref[...]) + sh_ref[...]
    z = _dot(h.astype(BF16), win_ref[...])
    tm = z.shape[0]
    u = jax.nn.gelu(z[:, :gw])
    vg = jax.nn.gelu(z[:, gw:2 * gw])
    hd = gw // gheads
    gv = gv_ref[...]
    v = jnp.concatenate([_rms(vg[:, i * hd:(i + 1) * hd], gv) for i in range(gheads)], axis=-1)
    v_ref[...] = v

    if sample:
        s = v * w00_ref[...] + b0_ref[...]
    else:
        chunk = ws_ref.shape[-1]
        row = lax.broadcasted_iota(jnp.int32, (chunk, chunk), 0)
        col = lax.broadcasted_iota(jnp.int32, (chunk, chunk), 1)
        vb = v.astype(BF16)
        cols = []
        for i in range(gheads):
            wt = jnp.where(col <= row, ws_ref[i], 0.0).astype(BF16)
            rows = [_dot(wt, vb[c * chunk:(c + 1) * chunk, i * hd:(i + 1) * hd]) + bs_ref[i]
                    for c in range(tm // chunk)]
            cols.append(jnp.concatenate(rows, axis=0) if len(rows) > 1 else rows[0])
        s = jnp.concatenate(cols, axis=-1)
    g = u * s
    gn_ref[...] = _rms(g, gmix_ref[...]).astype(gn_ref.dtype)

    o0 = 2 * gw
    cq = _rms(z[:, o0:o0 + qrank], gcq_ref[...])
    qq = _dot(cq.astype(BF16), wuq_ref[...])
    hw = M_HEADS * LANE
    cqt = cq_ref[...]
    sqt = sq_ref[...]
    gq = gq_ref[...]
    n_real = QK_NOPE + kr_ref.shape[-1]
    q_heads = []
    for i in range(M_HEADS):
        qh = qq[:, i * LANE:(i + 1) * LANE] * cqt + qq[:, hw + i * LANE:hw + (i + 1) * LANE] * sqt
        q_heads.append(_rms(qh, gq, n_real))
    q = jnp.concatenate(q_heads, axis=-1)
    q_ref[...] = q.astype(q_ref.dtype)

    o1 = o0 + qrank
    ckv = _rms(z[:, o1:o1 + kvrank], gckv_ref[...])
    ckv_ref[...] = ckv
    o2 = o1 + kvrank
    krp = z[:, o2:o2 + LANE] * cqt + z[:, o2 + LANE:o2 + 2 * LANE] * sqt
    rope = kr_ref.shape[-1]
    kr_ref[...] = krp[:, QK_NOPE:QK_NOPE + rope]
    kv = _dot(ckv.astype(BF16), wkv_ref[...])
    kr_ss = jnp.sum(krp * krp, axis=-1, keepdims=True)
    gk = gk_ref[...]
    k_heads = []
    for i in range(M_HEADS):
        kn = kv[:, i * LANE:(i + 1) * LANE]
        ss = jnp.sum(kn * kn, axis=-1, keepdims=True) + kr_ss
        k_heads.append((kn + krp) * lax.rsqrt(ss * (1.0 / n_real) + EPS) * gk)
    k_ref[...] = jnp.concatenate(k_heads, axis=-1).astype(k_ref.dtype)
    if sample:
        lane = lax.broadcasted_iota(jnp.int32, (1, LANE), 1)
        gk_nope = jnp.where(lane < QK_NOPE, gk, 0.0)
        qa = [_dot((q_heads[i] * gk_nope).astype(BF16), wukt_ref[i]) for i in range(M_HEADS)]
        qa_ref[...] = jnp.concatenate(qa, axis=-1)
    else:
        vv_ref[...] = kv[:, hw:hw + vw].astype(vv_ref.dtype)


def _mixer_in(x, mods, lw, tabs, *, sample, t_len, tm, rope):
    n, d = x.shape
    (g_norm_mix, w_in_p, g_v, g_cq, w_uq_p, g_ckv, w_kv_p, gq_p, gk_p, g_mix_g, extra) = lw
    cq_tab, sq_tab = tabs
    gw = g_mix_g.shape[-1]
    qrank = g_cq.shape[-1]
    kvrank = g_ckv.shape[-1]
    gheads = gw // g_v.shape[-1]
    assert w_in_p.shape[1] == 2 * gw + qrank + kvrank + 2 * LANE
    vw = w_kv_p.shape[1] - M_HEADS * LANE
    nt = n // tm
    full = lambda a: pl.BlockSpec(a.shape, lambda i: (0,) * a.ndim)
    if sample:
        mod_spec = lambda k: pl.BlockSpec((tm, d), lambda i, k=k: (i, k))
        tab_spec = pl.BlockSpec((1, LANE), lambda i: (0, 0))
    else:
        tpb = t_len // tm
        mod_spec = lambda k: pl.BlockSpec((None, 1, d), lambda i, k=k: (i // tpb, 0, k))
        tab_spec = pl.BlockSpec((tm, LANE), lambda i: (i % tpb, 0))
    row = lambda w: pl.BlockSpec((tm, w), lambda i: (i, 0))
    in_specs = [row(d), mod_spec(0), mod_spec(1), full(g_norm_mix), full(w_in_p), full(g_v), full(g_cq),
                full(w_uq_p), full(g_ckv), full(w_kv_p), full(gq_p), full(gk_p), tab_spec, tab_spec,
                full(g_mix_g)] + [full(a) for a in extra]
    kr_w = rope
    hw = M_HEADS * LANE
    if sample:
        out_shape = [jax.ShapeDtypeStruct((n, gw), BF16), jax.ShapeDtypeStruct((n, gw), F32),
                     jax.ShapeDtypeStruct((n, hw), F32), jax.ShapeDtypeStruct((n, kvrank), F32),
                     jax.ShapeDtypeStruct((n, kr_w), F32), jax.ShapeDtypeStruct((n, hw), F32),
                     jax.ShapeDtypeStruct((n, M_HEADS * kvrank), F32)]
        out_specs = [row(gw), row(gw), row(hw), row(kvrank), row(kr_w), row(hw), row(M_HEADS * kvrank)]
    else:
        out_shape = [jax.ShapeDtypeStruct((n, gw), BF16), jax.ShapeDtypeStruct((n, gw), F32),
                     jax.ShapeDtypeStruct((n, hw), BF16), jax.ShapeDtypeStruct((n, kvrank), F32),
                     jax.ShapeDtypeStruct((n, kr_w), F32), jax.ShapeDtypeStruct((n, hw), BF16),
                     jax.ShapeDtypeStruct((n, vw), BF16)]
        out_specs = [row(gw), row(gw), row(hw), row(kvrank), row(kr_w), row(hw), row(vw)]
    dims = (gw, qrank, kvrank, gheads, vw)
    return pl.pallas_call(
        functools.partial(_mixer_in_kernel, dims, sample),
        out_shape=out_shape,
        grid=(nt,),
        in_specs=in_specs,
        out_specs=out_specs,
        compiler_params=_cparams(("arbitrary",)),
        name="mixer_in_sample" if sample else "mixer_in_prompt",
    )(x, mods, mods, g_norm_mix, w_in_p, g_v, g_cq, w_uq_p, g_ckv, w_kv_p, gq_p, gk_p, cq_tab, sq_tab,
      g_mix_g, *extra)


def _prompt_attn_kernel(tq, vhead, q_ref, k_ref, v_ref, o_ref):
    qi = pl.program_id(2)
    heads = q_ref.shape[-1] // LANE
    qs = [q_ref[:, h * LANE:(h + 1) * LANE] for h in range(heads)]

    def scores(j, h):
        start = pl.multiple_of(j * tq, tq)
        k = k_ref[pl.ds(start, tq), h * LANE:(h + 1) * LANE]
        v = v_ref[pl.ds(start, tq), h * vhead:(h + 1) * vhead]
        return _dot_nt(qs[h], k), v

    def update(carry, s, v):
        m, l, acc = carry
        m_new = jnp.maximum(m, jnp.max(s, axis=-1, keepdims=True))
        alpha = jnp.exp(m - m_new)
        p = jnp.exp(s - m_new)
        l = l * alpha + jnp.sum(p, axis=-1, keepdims=True)
        acc = acc * alpha + _dot(p.astype(BF16), v)
        return m_new, l, acc

    def body(j, carry):
        return tuple(update(carry[h], *scores(j, h)) for h in range(heads))

    init = tuple((jnp.full((tq, 1), -jnp.inf, F32), jnp.zeros((tq, 1), F32), jnp.zeros((tq, vhead), F32))
                 for _ in range(heads))
    carry = lax.fori_loop(0, qi, body, init)
    causal = (lax.broadcasted_iota(jnp.int32, (tq, tq), 1) <= lax.broadcasted_iota(jnp.int32, (tq, tq), 0))
    outs = []
    for h in range(heads):
        s, v = scores(qi, h)
        m, l, acc = update(carry[h], jnp.where(causal, s, -jnp.inf), v)
        outs.append(acc / l)
    o_ref[...] = jnp.concatenate(outs, axis=-1).astype(o_ref.dtype)


def _prompt_attn(q, k, v, *, batch, t_len, tq):
    n = q.shape[0]
    vhead = v.shape[1] // M_HEADS
    hp = LANE // vhead
    nq = t_len // tq
    return pl.pallas_call(
        functools.partial(_prompt_attn_kernel, tq, vhead),
        out_shape=jax.ShapeDtypeStruct((n, v.shape[1]), F32),
        grid=(batch, M_HEADS // hp, nq),
        in_specs=[pl.BlockSpec((tq, hp * LANE), lambda b, h, i: (b * nq + i, h)),
                  pl.BlockSpec((t_len, hp * LANE), lambda b, h, i: (b, h)),
                  pl.BlockSpec((t_len, hp * vhead), lambda b, h, i: (b, h))],
        out_specs=pl.BlockSpec((tq, hp * vhead), lambda b, h, i: (b * nq + i, h)),
        compiler_params=_cparams(("arbitrary", "arbitrary", "arbitrary")),
        name="prompt_attention",
    )(q, k, v)


def _sample_attn_kernel(layer, n_pages, ppt, tpi, pt_ref, q_ref, qa_ref, knew_ref, cnew_ref, gk_ref, wukt_ref,
                        wuv_ref, ckv_hbm, krt_hbm, o_ref, cbuf, rbuf, sem_c, sem_r, wext_ref, s_scr, p_scr,
                        kt_a, kt_b):
    seq = pl.program_id(0)
    n_seq = pl.num_programs(0)
    slot = lax.rem(seq, 2)
    nxt_slot = 1 - slot
    nxt = jnp.minimum(seq + 1, n_seq - 1)
    rope, page = rbuf.shape[2], rbuf.shape[3]
    tile = ppt * page
    n_tiles = n_pages // ppt
    hn = wukt_ref.shape[0]
    n_real = QK_NOPE + rope

    def page_copies(sq, sl, i):
        pg = pt_ref[sq * n_pages + i]
        return (pltpu.make_async_copy(ckv_hbm.at[layer, pg], cbuf.at[sl, pl.ds(i * page, page)], sem_c.at[sl]),
                pltpu.make_async_copy(krt_hbm.at[layer, pg], rbuf.at[sl, i], sem_r.at[sl]))

    def start_page(sq, sl, i):
        for cp in page_copies(sq, sl, i):
            cp.start()

    def wait_pages(sq, sl):
        def body(i, carry):
            for cp in page_copies(sq, sl, i):
                cp.wait()
            return carry
        lax.fori_loop(0, n_pages, body, 0)

    @pl.when(seq == 0)
    def _():
        def body(i, carry):
            start_page(0, 0, i)
            return carry
        lax.fori_loop(0, n_pages, body, 0)

    wait_pages(seq, slot)

    q = q_ref[0]
    pad = wext_ref.shape[0] - hn - M_HEADS
    wext_ref[...] = jnp.concatenate(
        [wukt_ref[...], qa_ref[0].astype(BF16), jnp.zeros((pad, wext_ref.shape[1]), BF16)], axis=0)
    qr = (q * gk_ref[...])[:, QK_NOPE:QK_NOPE + rope].astype(BF16)

    def c_tile(t):
        return cbuf[slot, pl.ds(pl.multiple_of(t * tile, tile), tile), :].astype(BF16)

    def expand(g, kt_ref):
        for u in range(tpi):
            kt_ref[u] = _dot_nt(wext_ref[...], c_tile(g * tpi + u))

    def score(g, kt_ref):
        for u in range(tpi):
            t = g * tpi + u
            krt = jnp.concatenate([rbuf[slot, t * ppt + k] for k in range(ppt)], axis=1)
            kn = kt_ref[u, :hn, :]
            ss = jnp.sum((kn * kn).reshape(M_HEADS, QK_NOPE, tile), axis=1)
            kr_ss = jnp.sum(krt * krt, axis=0, keepdims=True)
            rinv = lax.rsqrt((ss + kr_ss) * (1.0 / n_real) + EPS)
            s_scr[t] = (kt_ref[u, hn:hn + M_HEADS, :] + _dot(qr, krt.astype(BF16))) * rinv

    n_groups = n_tiles // tpi
    n_pairs = (n_groups - 1) // 2
    for i in range(n_pages):
        start_page(nxt, nxt_slot, i)
    expand(0, kt_a)

    def pair(h, carry):
        g = 2 * h
        expand(g + 1, kt_b)
        score(g, kt_a)
        expand(g + 2, kt_a)
        score(g + 1, kt_b)
        return carry

    lax.fori_loop(0, n_pairs, pair, 0)
    g_tail = 2 * n_pairs
    if n_groups - g_tail == 2:
        expand(g_tail + 1, kt_b)
        score(g_tail, kt_a)
        score(g_tail + 1, kt_b)
    else:
        score(g_tail, kt_a)

    s_all = s_scr[...]
    s_new = jnp.sum(q * knew_ref[0], axis=-1, keepdims=True)
    m = jnp.maximum(jnp.max(jnp.max(s_all, axis=0), axis=-1, keepdims=True), s_new)
    p_all = jnp.exp(s_all - m)
    p_new = jnp.exp(s_new - m)
    l = jnp.sum(jnp.sum(p_all, axis=0), axis=-1, keepdims=True) + p_new
    p_scr[...] = p_all

    def phase_c(g, accs):
        return tuple(accs[u] + _dot(p_scr[g * tpi + u].astype(BF16), c_tile(g * tpi + u)) for u in range(tpi))

    zero = jnp.zeros((M_HEADS, cbuf.shape[-1]), F32)
    accs = lax.fori_loop(0, n_tiles // tpi, phase_c, (zero,) * tpi)
    acc = functools.reduce(lambda a, b: a + b, accs)
    c_new = cnew_ref[0].astype(BF16).astype(F32)
    o_lat = (acc + p_new.astype(BF16).astype(F32) * c_new) / l
    full = _dot(o_lat.astype(BF16), wuv_ref[...])
    vhead = full.shape[-1] // M_HEADS
    hrow = lax.broadcasted_iota(jnp.int32, full.shape, 0)
    hcol = lax.broadcasted_iota(jnp.int32, full.shape, 1) // vhead
    o_ref[0] = jnp.sum(jnp.where(hrow == hcol, full, 0.0), axis=0, keepdims=True)

    @pl.when(seq == n_seq - 1)
    def _():
        wait_pages(nxt, nxt_slot)


def _sample_attn(page_table, q, qa, k_new, c_new, gk_p, wukt, wuv, cache_kv, cache_krt, *, layer):
    s_n = q.shape[0]
    n_pages = page_table.shape[1]
    page = cache_kv.shape[2]
    kvrank = cache_kv.shape[3]
    rope = cache_krt.shape[2]
    ppt = max(1, MXU_TILE // page)
    assert n_pages % ppt == 0
    n_tiles = n_pages // ppt
    tpi = 4
    while n_tiles % tpi:
        tpi //= 2
    tile = ppt * page
    vw = wuv.shape[1]
    seq3 = lambda a, b: pl.BlockSpec((1, a, b), lambda s, pt: (s, 0, 0))
    full = lambda a: pl.BlockSpec(a.shape, lambda s, pt: (0,) * a.ndim)
    hbm = pl.BlockSpec(memory_space=pl.ANY)
    in_specs = [seq3(M_HEADS, LANE), seq3(M_HEADS, kvrank), seq3(M_HEADS, LANE), seq3(1, kvrank),
                full(gk_p), full(wukt), full(wuv), hbm, hbm]
    wext_rows = wukt.shape[0] + 2 * M_HEADS
    return pl.pallas_call(
        functools.partial(_sample_attn_kernel, layer, n_pages, ppt, tpi),
        out_shape=jax.ShapeDtypeStruct((s_n, 1, vw), F32),
        grid_spec=pltpu.PrefetchScalarGridSpec(
            num_scalar_prefetch=1, grid=(s_n,), in_specs=in_specs,
            out_specs=pl.BlockSpec((1, 1, vw), lambda s, pt: (s, 0, 0)),
            scratch_shapes=[pltpu.VMEM((2, n_pages * page, kvrank), F32),
                            pltpu.VMEM((2, n_pages, rope, page), F32),
                            pltpu.SemaphoreType.DMA((2,)), pltpu.SemaphoreType.DMA((2,)),
                            pltpu.VMEM((wext_rows, kvrank), BF16),
                            pltpu.VMEM((n_tiles, M_HEADS, tile), F32),
                            pltpu.VMEM((n_tiles, M_HEADS, tile), F32),
                            pltpu.VMEM((tpi, wext_rows, tile), F32),
                            pltpu.VMEM((tpi, wext_rows, tile), F32)]),
        compiler_params=_cparams(("arbitrary",)),
        name="sample_attention",
    )(page_table.reshape(-1), q.reshape(s_n, M_HEADS, LANE), qa.reshape(s_n, M_HEADS, kvrank),
      k_new.reshape(s_n, M_HEADS, LANE), c_new.reshape(s_n, 1, kvrank), gk_p, wukt, wuv,
      cache_kv, cache_krt).reshape(s_n, vw)


def _mixer_out_kernel(n_exp, gn_ref, a_ref, x_ref, gt_ref, sh_ref, sc_ref, gmix_ref, wout_ref, gffn_ref,
                      wr_ref, br_ref, cnt_in_ref, x1_ref, h2_ref, route_ref, wt_ref, cnt_ref, run_ref):
    @pl.when(pl.program_id(0) == 0)
    def _():
        run_ref[...] = cnt_in_ref[...]

    gw = gn_ref.shape[-1]
    an = _rms(a_ref[...].astype(F32), gmix_ref[...])
    y = _dot(gn_ref[...], wout_ref[:gw, :]) + _dot(an.astype(BF16), wout_ref[gw:, :])
    x1 = x_ref[...] + gt_ref[...] * y
    x1_ref[...] = x1
    h2 = _rms(x1, gffn_ref[...]) * (1.0 + sc_ref[...]) + sh_ref[...]
    _store_row_tiles(h2_ref, h2)
    tm = h2.shape[0]

    logits = _dot_nt(wr_ref[...], h2, precision=lax.Precision.HIGHEST)
    scores = jax.nn.sigmoid(logits)
    sel = scores + br_ref[...]
    per = n_exp // N_GROUPS
    best = None
    for g in range(N_GROUPS):
        a, b, c, d = [sel[g * per + i:g * per + i + 1, :] for i in range(per)]
        hi1, lo1 = jnp.maximum(a, b), jnp.minimum(a, b)
        hi2, lo2 = jnp.maximum(c, d), jnp.minimum(c, d)
        gs = jnp.maximum(hi1, hi2) + jnp.maximum(jnp.minimum(hi1, hi2), jnp.maximum(lo1, lo2))
        if best is None:
            best, grp = gs, jnp.zeros(gs.shape, jnp.int32)
        else:
            better = gs > best
            grp = jnp.where(better, g, grp)
            best = jnp.where(better, gs, best)
    erow = lax.broadcasted_iota(jnp.int32, (n_exp, tm), 0)
    selm = jnp.where(erow // per == grp, sel, -jnp.inf)
    m1 = jnp.max(selm, axis=0, keepdims=True)
    i1 = jnp.min(jnp.where(selm == m1, erow, n_exp), axis=0, keepdims=True)
    oh1 = erow == i1
    selm2 = jnp.where(oh1, -jnp.inf, selm)
    m2 = jnp.max(selm2, axis=0, keepdims=True)
    i2 = jnp.min(jnp.where(selm2 == m2, erow, n_exp), axis=0, keepdims=True)
    oh2 = erow == i2
    s1 = jnp.sum(jnp.where(oh1, scores, 0.0), axis=0, keepdims=True)
    s2 = jnp.sum(jnp.where(oh2, scores, 0.0), axis=0, keepdims=True)
    tot = s1 + s2
    w1, w2 = s1 / tot, s2 / tot

    oh = jnp.where(oh1 | oh2, 1.0, 0.0)
    tri = (lax.broadcasted_iota(jnp.int32, (tm, tm), 0)
           <= lax.broadcasted_iota(jnp.int32, (tm, tm), 1)).astype(BF16)
    before = _dot(oh.astype(BF16), tri) - oh + run_ref[:, 0:1]
    r1 = jnp.sum(jnp.where(oh1, before, 0.0), axis=0, keepdims=True)
    r2 = jnp.sum(jnp.where(oh2, before, 0.0), axis=0, keepdims=True)
    run = run_ref[...] + jnp.sum(oh, axis=1, keepdims=True)
    run_ref[...] = run
    cnt_ref[...] = run
    zero = jnp.zeros((1, tm), F32)
    route_ref[...] = jnp.concatenate([i1.astype(F32), i2.astype(F32), r1, r2, w1, w2, zero, zero], axis=0)
    wt_ref[...] = jnp.concatenate([w1, w2, jnp.zeros((LANE - 2, tm), F32)], axis=0).T


def _mixer_out(gn, attn, x, mods, g_mix_a, w_out_b, g_norm_ffn, w_router_t, b_router, cnt_in, *, sample, t_len,
               tm):
    n, d = x.shape
    gw = gn.shape[1]
    n_exp = w_router_t.shape[0]
    full = lambda a: pl.BlockSpec(a.shape, lambda i: (0,) * a.ndim)
    row = lambda w: pl.BlockSpec((tm, w), lambda i: (i, 0))
    if sample:
        mod_spec = lambda k: pl.BlockSpec((tm, d), lambda i, k=k: (i, k))
    else:
        tpb = t_len // tm
        mod_spec = lambda k: pl.BlockSpec((None, 1, d), lambda i, k=k: (i // tpb, 0, k))
    return pl.pallas_call(
        functools.partial(_mixer_out_kernel, n_exp),
        out_shape=[jax.ShapeDtypeStruct((n, d), F32), jax.ShapeDtypeStruct((n * (d // LANE), LANE), F32),
                   jax.ShapeDtypeStruct((8, n), F32), jax.ShapeDtypeStruct((n, LANE), F32),
                   jax.ShapeDtypeStruct((n_exp, LANE), F32)],
        grid=(n // tm,),
        in_specs=[row(gw), row(attn.shape[1]), row(d), mod_spec(2), mod_spec(3), mod_spec(4), full(g_mix_a),
                  full(w_out_b), full(g_norm_ffn), full(w_router_t), full(b_router), full(cnt_in)],
        out_specs=[row(d), pl.BlockSpec((tm * (d // LANE), LANE), lambda i: (i, 0)),
                   pl.BlockSpec((8, tm), lambda i: (0, i)), row(LANE),
                   pl.BlockSpec((n_exp, LANE), lambda i: (0, 0))],
        scratch_shapes=[pltpu.VMEM((n_exp, LANE), F32)],
        compiler_params=_cparams(("arbitrary",)),
        name="mixer_out_sample" if sample else "mixer_out_prompt",
    )(gn, attn, x, mods, mods, mods, g_mix_a, w_out_b, g_norm_ffn, w_router_t, b_router, cnt_in)


def _dispatch_kernel(tm, k, h_hbm, dest_hbm, xs_in, xs_hbm, hbuf, dsm, sem_h, sem_d, sem_r):
    del xs_in
    i = pl.program_id(0)
    n = pl.num_programs(0)
    slot = lax.rem(i, 2)
    other = 1 - slot

    def tile_loads(t, sl):
        rows = pl.ds(pl.multiple_of(t * (tm * k), tm * k), tm * k)
        return (pltpu.make_async_copy(h_hbm.at[rows], hbuf.at[sl], sem_h.at[sl]),
                pltpu.make_async_copy(dest_hbm.at[t], dsm.at[sl], sem_d.at[sl]))

    def row_copy(sl, r, c):
        dst = pl.multiple_of(dsm[sl, c * tm + r] * k, k)
        return pltpu.make_async_copy(hbuf.at[sl, pl.ds(r * k, k)], xs_hbm.at[pl.ds(dst, k)], sem_r.at[sl])

    def wait_rows(sl):
        for r in range(tm):
            row_copy(sl, r, 0).wait()
            row_copy(sl, r, 1).wait()

    @pl.when(i == 0)
    def _():
        for cp in tile_loads(0, 0):
            cp.start()

    @pl.when(i > 0)
    def _():
        wait_rows(other)

    @pl.when(i + 1 < n)
    def _():
        for cp in tile_loads(i + 1, other):
            cp.start()

    for cp in tile_loads(i, slot):
        cp.wait()
    for r in range(tm):
        row_copy(slot, r, 0).start(priority=0)
        row_copy(slot, r, 1).start(priority=1)

    @pl.when(i == n - 1)
    def _():
        wait_rows(slot)


def _dispatch(h2t, dest, xs, *, tm, k):
    n = h2t.shape[0] // k
    hbm = pl.BlockSpec(memory_space=pl.ANY)
    return pl.pallas_call(
        functools.partial(_dispatch_kernel, tm, k),
        out_shape=jax.ShapeDtypeStruct(xs.shape, xs.dtype),
        grid=(n // tm,),
        in_specs=[hbm, hbm, hbm],
        out_specs=hbm,
        scratch_shapes=[pltpu.VMEM((2, tm * k, LANE), F32), pltpu.SMEM((2, 2 * tm), jnp.int32),
                        pltpu.SemaphoreType.DMA((2,)), pltpu.SemaphoreType.DMA((2,)),
                        pltpu.SemaphoreType.DMA((2,))],
        input_output_aliases={2: 0},
        compiler_params=_cparams(("arbitrary",)),
        name="moe_dispatch",
    )(h2t, dest, xs)


def _experts_kernel(tm, te_ref, nu_ref, x_ref, wg_ref, wu_ref, wd_ref, y_ref):
    j = pl.program_id(0)

    @pl.when(j < nu_ref[0])
    def _():
        x = _load_row_tiles(x_ref, tm).astype(BF16)
        a = _dot(x, wg_ref[...].astype(BF16))
        b = _dot(x, wu_ref[...].astype(BF16))
        act = (a * jax.nn.sigmoid(a)) * b
        _store_row_tiles(y_ref, _dot(act.astype(BF16), wd_ref[...].astype(BF16)))

    @pl.when(j >= nu_ref[0])
    def _():
        y_ref[...] = jnp.zeros(y_ref.shape, y_ref.dtype)


def _experts(tile_expert, n_used, xs, w_gate, w_up, w_down, *, layer, tm):
    d, f = w_gate.shape[-2:]
    k = d // LANE
    p = xs.shape[0] // k
    x_spec = pl.BlockSpec((tm * k, LANE), lambda j, te, nu: (jnp.minimum(j, nu[0] - 1), 0))
    w_spec = lambda a, b: pl.BlockSpec((None, None, a, b), lambda j, te, nu: (layer, te[j], 0, 0))
    return pl.pallas_call(
        functools.partial(_experts_kernel, tm),
        out_shape=jax.ShapeDtypeStruct(xs.shape, F32),
        grid_spec=pltpu.PrefetchScalarGridSpec(
            num_scalar_prefetch=2, grid=(p // tm,),
            in_specs=[x_spec, w_spec(d, f), w_spec(d, f), w_spec(f, d)],
            out_specs=pl.BlockSpec((tm * k, LANE), lambda j, te, nu: (j, 0))),
        compiler_params=_cparams(("arbitrary",)),
        name="moe_experts",
    )(tile_expert, n_used, xs, w_gate, w_up, w_down)


def _combine_kernel(tm, k, x1_ref, gt_ref, wt_ref, dest_hbm, ys_hbm, o_ref, dsm, ybuf, sem_d, sem_r):
    i = pl.program_id(0)
    n = pl.num_programs(0)
    slot = lax.rem(i, 2)
    other = 1 - slot

    def dest_load(t):
        ds = lax.rem(t, 3)
        return pltpu.make_async_copy(dest_hbm.at[t], dsm.at[ds], sem_d.at[ds])

    def row_copy(t, sl, r, c):
        src = pl.multiple_of(dsm[lax.rem(t, 3), c * tm + r] * k, k)
        return pltpu.make_async_copy(ys_hbm.at[pl.ds(src, k)], ybuf.at[sl, c, pl.ds(r * k, k)], sem_r.at[sl])

    def start_rows(t, sl):
        for r in range(tm):
            row_copy(t, sl, r, 0).start(priority=0)
            row_copy(t, sl, r, 1).start(priority=1)

    @pl.when(i == 0)
    def _():
        dest_load(0).start()

        @pl.when(n > 1)
        def _():
            dest_load(1).start()

        dest_load(0).wait()
        start_rows(0, 0)

    @pl.when(i + 2 < n)
    def _():
        dest_load(i + 2).start()

    @pl.when(i + 1 < n)
    def _():
        dest_load(i + 1).wait()
        start_rows(i + 1, other)

    for r in range(tm):
        row_copy(i, slot, r, 0).wait()
        row_copy(i, slot, r, 1).wait()
    wt = wt_ref[...]
    y = (wt[:, 0:1] * _load_row_tiles(ybuf.at[slot, 0], tm) + wt[:, 1:2] * _load_row_tiles(ybuf.at[slot, 1], tm))
    o_ref[...] = x1_ref[...] + gt_ref[...] * y


def _combine(x1, mods, wt, dest, ys, *, sample, t_len, tm):
    n, d = x1.shape
    k = d // LANE
    row = lambda w: pl.BlockSpec((tm, w), lambda i: (i, 0))
    if sample:
        mod_spec = pl.BlockSpec((tm, d), lambda i: (i, 5))
    else:
        tpb = t_len // tm
        mod_spec = pl.BlockSpec((None, 1, d), lambda i: (i // tpb, 0, 5))
    return pl.pallas_call(
        functools.partial(_combine_kernel, tm, k),
        out_shape=jax.ShapeDtypeStruct((n, d), F32),
        grid=(n // tm,),
        in_specs=[row(d), mod_spec, row(LANE), pl.BlockSpec(memory_space=pl.ANY),
                  pl.BlockSpec(memory_space=pl.ANY)],
        out_specs=row(d),
        scratch_shapes=[pltpu.SMEM((3, 2 * tm), jnp.int32), pltpu.VMEM((2, 2, tm * k, LANE), F32),
                        pltpu.SemaphoreType.DMA((3,)), pltpu.SemaphoreType.DMA((2,))],
        compiler_params=_cparams(("arbitrary",)),
        name="moe_combine_sample" if sample else "moe_combine_prompt",
    )(x1, mods, wt, dest, ys)


def _route_tables(counts, routes, tms, tm_e, n_tiles):
    n_exp = counts.shape[0]
    cnt = counts.astype(jnp.int32)
    padded = ((cnt + tm_e - 1) // tm_e) * tm_e
    ends = jnp.cumsum(padded)
    off = ends - padded
    n_used = (ends[-1] // tm_e).reshape(1)
    tile_start = jnp.arange(n_tiles, dtype=jnp.int32) * tm_e
    tile_expert = jnp.minimum(jnp.sum(tile_start[:, None] >= ends[None, :], axis=1), n_exp - 1).astype(jnp.int32)
    eids = jnp.arange(n_exp, dtype=jnp.int32)[:, None]
    dests = []
    for route, tm in zip(routes, tms):
        n = route.shape[1]
        e = route[0:2].astype(jnp.int32)
        base = jnp.sum(jnp.where(e[:, None, :] == eids[None], off[None, :, None], 0), axis=1)
        dest = base + route[2:4].astype(jnp.int32)
        dests.append(dest.reshape(2, n // tm, tm).transpose(1, 0, 2).reshape(n // tm, 2 * tm))
    return tile_expert, n_used, dests


def _rope_tables(pos, rope):
    half = rope // 2
    freqs = ROPE_THETA ** (-jnp.arange(half, dtype=F32) / half)
    ang = pos.astype(F32)[:, None] * freqs[None, :]
    cos, sin = jnp.cos(ang), jnp.sin(ang)
    t = pos.shape[0]
    pad = LANE - QK_NOPE - rope
    cq = jnp.concatenate([jnp.ones((t, QK_NOPE), F32), cos, cos, jnp.zeros((t, pad), F32)], axis=-1)
    sq = jnp.concatenate([jnp.zeros((t, QK_NOPE), F32), sin, sin, jnp.zeros((t, pad), F32)], axis=-1)
    return cq, sq


def _rot_cols(w, rope):
    half = rope // 2
    return jnp.concatenate([-w[..., half:], w[..., :half]], axis=-1)


def _prep_layer(l, w_in, w_uq, w_ukv, g_qk_q, g_qk_k, rope, gw, qrank, kvrank, vhead):
    d = w_in.shape[1]
    qk_head = QK_NOPE + rope
    pad = LANE - qk_head
    o = 2 * gw + qrank + kvrank
    w_kr = w_in[l][:, o:o + rope]
    z64 = jnp.zeros((d, QK_NOPE), F32)
    zp = jnp.zeros((d, pad), F32)
    w_in_p = jnp.concatenate([w_in[l][:, :o], z64, w_kr, zp, z64, _rot_cols(w_kr, rope), zp], axis=-1).astype(BF16)
    wq = w_uq[l].reshape(qrank, M_HEADS, qk_head)
    zq = jnp.zeros((qrank, M_HEADS, pad), F32)
    wq_plain = jnp.concatenate([wq, zq], axis=-1).reshape(qrank, M_HEADS * LANE)
    wq_rot = jnp.concatenate([jnp.zeros((qrank, M_HEADS, QK_NOPE), F32), _rot_cols(wq[..., QK_NOPE:], rope), zq],
                             axis=-1).reshape(qrank, M_HEADS * LANE)
    w_uq_p = jnp.concatenate([wq_plain, wq_rot], axis=-1).astype(BF16)
    wkv = w_ukv[l].reshape(kvrank, M_HEADS, QK_NOPE + vhead)
    wk = wkv[..., :QK_NOPE]
    wk_pad = jnp.concatenate([wk, jnp.zeros((kvrank, M_HEADS, LANE - QK_NOPE), F32)], axis=-1)
    wv = wkv[..., QK_NOPE:].reshape(kvrank, M_HEADS * vhead)
    w_kv_p = jnp.concatenate([wk_pad.reshape(kvrank, M_HEADS * LANE), wv], axis=-1).astype(BF16)
    wukt_pad = jnp.concatenate([wk.transpose(1, 2, 0), jnp.zeros((M_HEADS, LANE - QK_NOPE, kvrank), F32)],
                               axis=1).astype(BF16)
    wukt = wk.transpose(1, 2, 0).reshape(M_HEADS * QK_NOPE, kvrank).astype(BF16)
    zg = jnp.zeros((pad,), F32)
    gq_p = (jnp.concatenate([g_qk_q[l], zg]) * (qk_head ** -0.5)).reshape(1, LANE)
    gk_p = jnp.concatenate([g_qk_k[l], zg]).reshape(1, LANE)
    return w_in_p, w_uq_p, w_kv_p, wukt_pad, wukt, wv.astype(BF16), gq_p, gk_p


def kernel(x_prompt, x_sample, c_prompt, c_sample, cache_kv_latent, cache_k_rope, page_table, w_ada, b_ada,
           g_norm_mix, g_norm_ffn, w_in, g_v, w_s, b_s, g_cq, w_uq, g_ckv, w_ukv, g_qk_q, g_qk_k, g_mix_out,
           w_out, w_router, b_router, w_gate, w_up, w_down):
    batch, t_p, d = x_prompt.shape
    s_n, t_s, _ = x_sample.shape
    depth = w_ada.shape[0]
    gheads, chunk = w_s.shape[1], w_s.shape[2]
    hd = g_v.shape[-1]
    gw = gheads * hd
    qrank = g_cq.shape[-1]
    kvrank = g_ckv.shape[-1]
    rope = cache_k_rope.shape[-1]
    vhead = w_ukv.shape[-1] // M_HEADS - QK_NOPE
    past = page_table.shape[1] * cache_kv_latent.shape[2]
    assert t_s == 1 and t_p % chunk == 0 and g_qk_q.shape[-1] == QK_NOPE + rope
    row2 = lambda a: a.reshape(1, -1)

    mods = _ada(jnp.concatenate([c_prompt, c_sample], axis=0), w_ada, b_ada)
    tab_p = _rope_tables(jnp.arange(t_p, dtype=jnp.int32), rope)
    tab_s = _rope_tables(past + jnp.arange(t_s, dtype=jnp.int32), rope)
    w_router_t = w_router.T
    b_router_c = b_router.reshape(-1, 1)

    tm_p = 512 if t_p % 512 == 0 else 256
    tm_d = 256
    tm_s = s_n
    n_exp = w_router.shape[1]
    tm_e = 512
    n_pairs = 2 * (batch * t_p + s_n)
    n_tiles_e = -(-n_pairs // tm_e) + n_exp
    tq = 512 if t_p % 512 == 0 else 256
    cache_krt = jnp.swapaxes(cache_k_rope, 2, 3)
    xp = x_prompt.reshape(batch * t_p, d)
    xs = x_sample.reshape(s_n, d)
    open_p = ((t_p - 1) // chunk) * chunk
    outs = [[] for _ in range(6)]
    for l in range(depth):
        w_in_p, w_uq_p, w_kv_p, wukt_pad, wukt, wuv, gq_p, gk_p = _prep_layer(
            l, w_in, w_uq, w_ukv, g_qk_q, g_qk_k, rope, gw, qrank, kvrank, vhead)
        g_mix_g, g_mix_a = row2(g_mix_out[l, :gw]), row2(g_mix_out[l, gw:])
        w_out_b = w_out[l].astype(BF16)
        base = (row2(g_norm_mix[l]), w_in_p, row2(g_v[l]), row2(g_cq[l]), w_uq_p, row2(g_ckv[l]), w_kv_p,
                gq_p, gk_p, g_mix_g)
        mp = mods[l, :batch].reshape(batch, 1, N_MOD * d)
        ms = mods[l, batch:]

        w00 = row2(jnp.repeat(w_s[l, :, 0, 0], hd))
        b0 = row2(jnp.repeat(b_s[l, :, 0], hd))
        gn, v, q, ckv, kr, k, qa = _mixer_in(xs, ms, base + ((w00, b0, wukt_pad),), tab_s,
                                             sample=True, t_len=t_s, tm=tm_s, rope=rope)
        attn = _sample_attn(page_table, q, qa, k, ckv, gk_p, wukt, wuv, cache_kv_latent, cache_krt, layer=l)
        x1s, h2s, route_s, wt_s, cnt = _mixer_out(gn, attn, xs, ms, g_mix_a, w_out_b, row2(g_norm_ffn[l]),
                                                  w_router_t, b_router_c, jnp.zeros((n_exp, LANE), F32),
                                                  sample=True, t_len=t_s, tm=tm_s)
        outs[3].append(ckv.reshape(s_n, t_s, kvrank))
        outs[4].append(kr.reshape(s_n, t_s, rope))
        outs[5].append(v.reshape(s_n, t_s, gheads, hd))

        bs_full = jnp.broadcast_to(b_s[l][:, :, None], (gheads, chunk, hd))
        gn, v, q, ckv, kr, k, vv = _mixer_in(xp, mp, base + ((w_s[l], bs_full),), tab_p,
                                             sample=False, t_len=t_p, tm=tm_p, rope=rope)
        attn = _prompt_attn(q, k, vv, batch=batch, t_len=t_p, tq=tq)
        x1p, h2p, route_p, wt_p, cnt = _mixer_out(gn, attn, xp, mp, g_mix_a, w_out_b, row2(g_norm_ffn[l]),
                                                  w_router_t, b_router_c, cnt, sample=False, t_len=t_p, tm=tm_p)
        outs[0].append(ckv.reshape(batch, t_p, kvrank))
        outs[1].append(kr.reshape(batch, t_p, rope))
        outs[2].append(v.reshape(batch, t_p, gw)[:, open_p:].reshape(batch, t_p - open_p, gheads, hd))

        tile_expert, n_used, (dest_s, dest_p) = _route_tables(
            cnt[:, 0], (route_s, route_p), (tm_s, tm_d), tm_e, n_tiles_e)
        if l == 0:
            xsort = jnp.zeros((n_tiles_e * tm_e * (d // LANE), LANE), F32)
        xsort = _dispatch(h2s, dest_s, xsort, tm=tm_s, k=d // LANE)
        xsort = _dispatch(h2p, dest_p, xsort, tm=tm_d, k=d // LANE)
        ysort = _experts(tile_expert, n_used, xsort, w_gate, w_up, w_down, layer=l, tm=tm_e)
        xs = _combine(x1s, ms, wt_s, dest_s, ysort, sample=True, t_len=t_s, tm=tm_s)
        xp = _combine(x1p, mp, wt_p, dest_p, ysort, sample=False, t_len=t_p, tm=tm_d)

    return (xp.reshape(batch, t_p, d), xs.reshape(s_n, t_s, d), jnp.stack(outs[0]), jnp.stack(outs[1]),
            jnp.stack(outs[2]), jnp.stack(outs[3]), jnp.stack(outs[4]), jnp.stack(outs[5]))
```

```python
import functools

import jax
import jax.numpy as jnp
from jax import lax
from jax.experimental import pallas as pl
from jax.experimental.pallas import tpu as pltpu

F32 = jnp.float32
BF16 = jnp.bfloat16

M_HEADS = 8
QK_NOPE = 64
N_GROUPS = 4
N_MOD = 6
ROPE_THETA = 10000.0
EPS = 1e-6
LANE = 128
MXU_TILE = 256
VMEM_LIMIT = 56 * 1024 * 1024


def _cparams(sem):
    return pltpu.CompilerParams(dimension_semantics=sem, vmem_limit_bytes=VMEM_LIMIT)


def _dot(a, b):
    return jnp.dot(a, b, preferred_element_type=F32)


def _dot_nt(a, b, precision=None):
    return lax.dot_general(a, b, (((1,), (1,)), ((), ())), preferred_element_type=F32,
                           precision=precision)


def _load_row_tiles(ref, rows):
    k = ref.shape[0] // rows
    return jnp.concatenate([ref[pl.ds(c, rows, stride=k), :] for c in range(k)], axis=-1)


def _store_row_tiles(ref, x):
    rows = x.shape[0]
    k = ref.shape[0] // rows
    for c in range(k):
        ref[pl.ds(c, rows, stride=k), :] = x[:, c * LANE:(c + 1) * LANE]


def _rms(x, g, n=None):
    n = x.shape[-1] if n is None else n
    ms = jnp.sum(x * x, axis=-1, keepdims=True) * (1.0 / n)
    return x * lax.rsqrt(ms + EPS) * g


def _ada_kernel(c_ref, w_ref, b_ref, o_ref):
    c = c_ref[...]
    s = c * jax.nn.sigmoid(c)
    o_ref[0] = _dot(s.astype(BF16), w_ref[0].astype(BF16)) + b_ref[0]


def _ada(c_all, w_ada, b_ada):
    depth, d, n6 = w_ada.shape
    rows = c_all.shape[0]
    tn = 1536
    return pl.pallas_call(
        _ada_kernel,
        out_shape=jax.ShapeDtypeStruct((depth, rows, n6), F32),
        grid=(depth, n6 // tn),
        in_specs=[pl.BlockSpec((rows, d), lambda l, j: (0, 0)),
                  pl.BlockSpec((1, d, tn), lambda l, j: (l, 0, j)),
                  pl.BlockSpec((1, 1, tn), lambda l, j: (l, 0, j))],
        out_specs=pl.BlockSpec((1, rows, tn), lambda l, j: (l, 0, j)),
        compiler_params=_cparams(("arbitrary", "arbitrary")),
        name="ada_modulation",
    )(c_all, w_ada, b_ada.reshape(depth, 1, n6))


def _mixer_in_kernel(dims, sample, *refs):
    gw, qrank, kvrank, gheads, vw = dims
    (x_ref, sh_ref, sc_ref, gnm_ref, win_ref, gv_ref, gcq_ref, wuq_ref, gckv_ref, wkv_ref,
     gq_ref, gk_ref, cq_ref, sq_ref, gmix_ref) = refs[:15]
    if sample:
        w00_ref, b0_ref, wukt_ref = refs[15:18]
        gn_ref, v_ref, q_ref, ckv_ref, kr_ref, k_ref, qa_ref = refs[18:]
    else:
        ws_ref, bs_ref = refs[15:17]
        gn_ref, v_ref, q_ref, ckv_ref, kr_ref, k_ref, vv_ref = refs[17:]

    x = x_ref[...]
    h = _rms(x, gnm_ref[...]) * (1.0 + sc_ref[...]) + sh_ref[...]
    z = _dot(h.astype(BF16), win_ref[...])
    tm = z.shape[0]
    u = jax.nn.gelu(z[:, :gw])
    vg = jax.nn.gelu(z[:, gw:2 * gw])
    hd = gw // gheads
    gv = gv_ref[...]
    v = jnp.concatenate([_rms(vg[:, i * hd:(i + 1) * hd], gv) for i in range(gheads)], axis=-1)
    v_ref[...] = v

    if sample:
        s = v * w00_ref[...] + b0_ref[...]
    else:
        chunk = ws_ref.shape[-1]
        row = lax.broadcasted_iota(jnp.int32, (chunk, chunk), 0)
        col = lax.broadcasted_iota(jnp.int32, (chunk, chunk), 1)
        vb = v.astype(BF16)
        cols = []
        for i in range(gheads):
            wt = jnp.where(col <= row, ws_ref[i], 0.0).astype(BF16)
            rows = [_dot(wt, vb[c * chunk:(c + 1) * chunk, i * hd:(i + 1) * hd]) + bs_ref[i]
                    for c in range(tm // chunk)]
            cols.append(jnp.concatenate(rows, axis=0) if len(rows) > 1 else rows[0])
        s = jnp.concatenate(cols, axis=-1)
    g = u * s
    gn_ref[...] = _rms(g, gmix_ref[...]).astype(gn_ref.dtype)

    o0 = 2 * gw
    cq = _rms(z[:, o0:o0 + qrank], gcq_ref[...])
    qq = _dot(cq.astype(BF16), wuq_ref[...])
    hw = M_HEADS * LANE
    cqt = cq_ref[...]
    sqt = sq_ref[...]
    gq = gq_ref[...]
    n_real = QK_NOPE + kr_ref.shape[-1]
    q_heads = []
    for i in range(M_HEADS):
        qh = qq[:, i * LANE:(i + 1) * LANE] * cqt + qq[:, hw + i * LANE:hw + (i + 1) * LANE] * sqt
        q_heads.append(_rms(qh, gq, n_real))
    q = jnp.concatenate(q_heads, axis=-1)
    q_ref[...] = q.astype(q_ref.dtype)

    o1 = o0 + qrank
    ckv = _rms(z[:, o1:o1 + kvrank], gckv_ref[...])
    ckv_ref[...] = ckv
    o2 = o1 + kvrank
    krp = z[:, o2:o2 + LANE] * cqt + z[:, o2 + LANE:o2 + 2 * LANE] * sqt
    rope = kr_ref.shape[-1]
    kr_ref[...] = krp[:, QK_NOPE:QK_NOPE + rope]
    kv = _dot(ckv.astype(BF16), wkv_ref[...])
    kr_ss = jnp.sum(krp * krp, axis=-1, keepdims=True)
    gk = gk_ref[...]
    k_heads = []
    for i in range(M_HEADS):
        kn = kv[:, i * LANE:(i + 1) * LANE]
        ss = jnp.sum(kn * kn, axis=-1, keepdims=True) + kr_ss
        k_heads.append((kn + krp) * lax.rsqrt(ss * (1.0 / n_real) + EPS) * gk)
    k_ref[...] = jnp.concatenate(k_heads, axis=-1).astype(k_ref.dtype)
    if sample:
        lane = lax.broadcasted_iota(jnp.int32, (1, LANE), 1)
        gk_nope = jnp.where(lane < QK_NOPE, gk, 0.0)
        qa = [_dot((q_heads[i] * gk_nope).astype(BF16), wukt_ref[i]) for i in range(M_HEADS)]
        qa_ref[...] = jnp.concatenate(qa, axis=-1)
    else:
        vv_ref[...] = kv[:, hw:hw + vw].astype(vv_ref.dtype)


def _mixer_in(x, mods, lw, tabs, *, sample, t_len, tm, rope):
    n, d = x.shape
    (g_norm_mix, w_in_p, g_v, g_cq, w_uq_p, g_ckv, w_kv_p, gq_p, gk_p, g_mix_g, extra) = lw
    cq_tab, sq_tab = tabs
    gw = g_mix_g.shape[-1]
    qrank = g_cq.shape[-1]
    kvrank = g_ckv.shape[-1]
    gheads = gw // g_v.shape[-1]
    assert w_in_p.shape[1] == 2 * gw + qrank + kvrank + 2 * LANE
    vw = w_kv_p.shape[1] - M_HEADS * LANE
    nt = n // tm
    full = lambda a: pl.BlockSpec(a.shape, lambda i: (0,) * a.ndim)
    if sample:
        mod_spec = lambda k: pl.BlockSpec((tm, d), lambda i, k=k: (i, k))
        tab_spec = pl.BlockSpec((1, LANE), lambda i: (0, 0))
    else:
        tpb = t_len // tm
        mod_spec = lambda k: pl.BlockSpec((None, 1, d), lambda i, k=k: (i // tpb, 0, k))
        tab_spec = pl.BlockSpec((tm, LANE), lambda i: (i % tpb, 0))
    row = lambda w: pl.BlockSpec((tm, w), lambda i: (i, 0))
    in_specs = [row(d), mod_spec(0), mod_spec(1), full(g_norm_mix), full(w_in_p), full(g_v), full(g_cq),
                full(w_uq_p), full(g_ckv), full(w_kv_p), full(gq_p), full(gk_p), tab_spec, tab_spec,
                full(g_mix_g)] + [full(a) for a in extra]
    kr_w = rope
    hw = M_HEADS * LANE
    if sample:
        out_shape = [jax.ShapeDtypeStruct((n, gw), BF16), jax.ShapeDtypeStruct((n, gw), F32),
                     jax.ShapeDtypeStruct((n, hw), F32), jax.ShapeDtypeStruct((n, kvrank), F32),
                     jax.ShapeDtypeStruct((n, kr_w), F32), jax.ShapeDtypeStruct((n, hw), F32),
                     jax.ShapeDtypeStruct((n, M_HEADS * kvrank), F32)]
        out_specs = [row(gw), row(gw), row(hw), row(kvrank), row(kr_w), row(hw), row(M_HEADS * kvrank)]
    else:
        out_shape = [jax.ShapeDtypeStruct((n, gw), BF16), jax.ShapeDtypeStruct((n, gw), F32),
                     jax.ShapeDtypeStruct((n, hw), BF16), jax.ShapeDtypeStruct((n, kvrank), F32),
                     jax.ShapeDtypeStruct((n, kr_w), F32), jax.ShapeDtypeStruct((n, hw), BF16),
                     jax.ShapeDtypeStruct((n, vw), BF16)]
        out_specs = [row(gw), row(gw), row(hw), row(kvrank), row(kr_w), row(hw), row(vw)]
    dims = (gw, qrank, kvrank, gheads, vw)
    return pl.pallas_call(
        functools.partial(_mixer_in_kernel, dims, sample),
        out_shape=out_shape,
        grid=(nt,),
        in_specs=in_specs,
        out_specs=out_specs,
        compiler_params=_cparams(("arbitrary",)),
        name="mixer_in_sample" if sample else "mixer_in_prompt",
    )(x, mods, mods, g_norm_mix, w_in_p, g_v, g_cq, w_uq_p, g_ckv, w_kv_p, gq_p, gk_p, cq_tab, sq_tab,
      g_mix_g, *extra)


def _prompt_attn_kernel(tq, vhead, q_ref, k_ref, v_ref, o_ref):
    qi = pl.program_id(2)
    heads = q_ref.shape[-1] // LANE
    qs = [q_ref[:, h * LANE:(h + 1) * LANE] for h in range(heads)]

    def scores(j, h):
        start = pl.multiple_of(j * tq, tq)
        k = k_ref[pl.ds(start, tq), h * LANE:(h + 1) * LANE]
        v = v_ref[pl.ds(start, tq), h * vhead:(h + 1) * vhead]
        return _dot_nt(qs[h], k), v

    def update(carry, s, v):
        m, l, acc = carry
        m_new = jnp.maximum(m, jnp.max(s, axis=-1, keepdims=True))
        alpha = jnp.exp(m - m_new)
        p = jnp.exp(s - m_new)
        l = l * alpha + jnp.sum(p, axis=-1, keepdims=True)
        acc = acc * alpha + _dot(p.astype(BF16), v)
        return m_new, l, acc

    def body(j, carry):
        return tuple(update(carry[h], *scores(j, h)) for h in range(heads))

    init = tuple((jnp.full((tq, 1), -jnp.inf, F32), jnp.zeros((tq, 1), F32), jnp.zeros((tq, vhead), F32))
                 for _ in range(heads))
    carry = lax.fori_loop(0, qi, body, init)
    causal = (lax.broadcasted_iota(jnp.int32, (tq, tq), 1) <= lax.broadcasted_iota(jnp.int32, (tq, tq), 0))
    outs = []
    for h in range(heads):
        s, v = scores(qi, h)
        m, l, acc = update(carry[h], jnp.where(causal, s, -jnp.inf), v)
        outs.append(acc / l)
    o_ref[...] = jnp.concatenate(outs, axis=-1).astype(o_ref.dtype)


def _prompt_attn(q, k, v, *, batch, t_len, tq):
    n = q.shape[0]
    vhead = v.shape[1] // M_HEADS
    hp = LANE // vhead
    nq = t_len // tq
    return pl.pallas_call(
        functools.partial(_prompt_attn_kernel, tq, vhead),
        out_shape=jax.ShapeDtypeStruct((n, v.shape[1]), F32),
        grid=(batch, M_HEADS // hp, nq),
        in_specs=[pl.BlockSpec((tq, hp * LANE), lambda b, h, i: (b * nq + i, h)),
                  pl.BlockSpec((t_len, hp * LANE), lambda b, h, i: (b, h)),
                  pl.BlockSpec((t_len, hp * vhead), lambda b, h, i: (b, h))],
        out_specs=pl.BlockSpec((tq, hp * vhead), lambda b, h, i: (b * nq + i, h)),
        compiler_params=_cparams(("arbitrary", "arbitrary", "arbitrary")),
        name="prompt_attention",
    )(q, k, v)


def _sample_attn_kernel(layer, n_pages, ppt, tpi, pt_ref, q_ref, qa_ref, knew_ref, cnew_ref, gk_ref, wukt_ref,
                        wuv_ref, ckv_hbm, krt_hbm, o_ref, cbuf, rbuf, sem_c, sem_r, wext_ref, s_scr, p_scr,
                        kt_a, kt_b):
    seq = pl.program_id(0)
    n_seq = pl.num_programs(0)
    slot = lax.rem(seq, 2)
    nxt_slot = 1 - slot
    nxt = jnp.minimum(seq + 1, n_seq - 1)
    rope, page = rbuf.shape[2], rbuf.shape[3]
    tile = ppt * page
    n_tiles = n_pages // ppt
    hn = wukt_ref.shape[0]
    n_real = QK_NOPE + rope

    def page_copies(sq, sl, i):
        pg = pt_ref[sq * n_pages + i]
        return (pltpu.make_async_copy(ckv_hbm.at[layer, pg], cbuf.at[sl, pl.ds(i * page, page)], sem_c.at[sl]),
                pltpu.make_async_copy(krt_hbm.at[layer, pg], rbuf.at[sl, i], sem_r.at[sl]))

    def start_page(sq, sl, i):
        for cp in page_copies(sq, sl, i):
            cp.start()

    def wait_pages(sq, sl):
        def body(i, carry):
            for cp in page_copies(sq, sl, i):
                cp.wait()
            return carry
        lax.fori_loop(0, n_pages, body, 0)

    @pl.when(seq == 0)
    def _():
        def body(i, carry):
            start_page(0, 0, i)
            return carry
        lax.fori_loop(0, n_pages, body, 0)

    wait_pages(seq, slot)

    q = q_ref[0]
    pad = wext_ref.shape[0] - hn - M_HEADS
    wext_ref[...] = jnp.concatenate(
        [wukt_ref[...], qa_ref[0].astype(BF16), jnp.zeros((pad, wext_ref.shape[1]), BF16)], axis=0)
    qr = (q * gk_ref[...])[:, QK_NOPE:QK_NOPE + rope].astype(BF16)

    def c_tile(t):
        return cbuf[slot, pl.ds(pl.multiple_of(t * tile, tile), tile), :].astype(BF16)

    def expand(g, kt_ref):
        for u in range(tpi):
            kt_ref[u] = _dot_nt(wext_ref[...], c_tile(g * tpi + u))

    def score(g, kt_ref):
        for u in range(tpi):
            t = g * tpi + u
            for k in range(ppt):
                start_page(nxt, nxt_slot, t * ppt + k)
            krt = jnp.concatenate([rbuf[slot, t * ppt + k] for k in range(ppt)], axis=1)
            kn = kt_ref[u, :hn, :]
            ss = jnp.sum((kn * kn).reshape(M_HEADS, QK_NOPE, tile), axis=1)
            kr_ss = jnp.sum(krt * krt, axis=0, keepdims=True)
            rinv = lax.rsqrt((ss + kr_ss) * (1.0 / n_real) + EPS)
            s_scr[t] = (kt_ref[u, hn:hn + M_HEADS, :] + _dot(qr, krt.astype(BF16))) * rinv

    n_groups = n_tiles // tpi
    n_pairs = (n_groups - 1) // 2
    expand(0, kt_a)

    def pair(h, carry):
        g = 2 * h
        expand(g + 1, kt_b)
        score(g, kt_a)
        expand(g + 2, kt_a)
        score(g + 1, kt_b)
        return carry

    lax.fori_loop(0, n_pairs, pair, 0)
    g_tail = 2 * n_pairs
    if n_groups - g_tail == 2:
        expand(g_tail + 1, kt_b)
        score(g_tail, kt_a)
        score(g_tail + 1, kt_b)
    else:
        score(g_tail, kt_a)

    s_all = s_scr[...]
    s_new = jnp.sum(q * knew_ref[0], axis=-1, keepdims=True)
    m = jnp.maximum(jnp.max(jnp.max(s_all, axis=0), axis=-1, keepdims=True), s_new)
    p_all = jnp.exp(s_all - m)
    p_new = jnp.exp(s_new - m)
    l = jnp.sum(jnp.sum(p_all, axis=0), axis=-1, keepdims=True) + p_new
    p_scr[...] = p_all

    def phase_c(g, accs):
        return tuple(accs[u] + _dot(p_scr[g * tpi + u].astype(BF16), c_tile(g * tpi + u)) for u in range(tpi))

    zero = jnp.zeros((M_HEADS, cbuf.shape[-1]), F32)
    accs = lax.fori_loop(0, n_tiles // tpi, phase_c, (zero,) * tpi)
    acc = functools.reduce(lambda a, b: a + b, accs)
    c_new = cnew_ref[0].astype(BF16).astype(F32)
    o_lat = (acc + p_new.astype(BF16).astype(F32) * c_new) / l
    full = _dot(o_lat.astype(BF16), wuv_ref[...])
    vhead = full.shape[-1] // M_HEADS
    hrow = lax.broadcasted_iota(jnp.int32, full.shape, 0)
    hcol = lax.broadcasted_iota(jnp.int32, full.shape, 1) // vhead
    o_ref[0] = jnp.sum(jnp.where(hrow == hcol, full, 0.0), axis=0, keepdims=True)

    @pl.when(seq == n_seq - 1)
    def _():
        wait_pages(nxt, nxt_slot)


def _sample_attn(page_table, q, qa, k_new, c_new, gk_p, wukt, wuv, cache_kv, cache_krt, *, layer):
    s_n = q.shape[0]
    n_pages = page_table.shape[1]
    page = cache_kv.shape[2]
    kvrank = cache_kv.shape[3]
    rope = cache_krt.shape[2]
    ppt = max(1, MXU_TILE // page)
    assert n_pages % ppt == 0
    n_tiles = n_pages // ppt
    tpi = 4
    while n_tiles % tpi:
        tpi //= 2
    tile = ppt * page
    vw = wuv.shape[1]
    seq3 = lambda a, b: pl.BlockSpec((1, a, b), lambda s, pt: (s, 0, 0))
    full = lambda a: pl.BlockSpec(a.shape, lambda s, pt: (0,) * a.ndim)
    hbm = pl.BlockSpec(memory_space=pl.ANY)
    in_specs = [seq3(M_HEADS, LANE), seq3(M_HEADS, kvrank), seq3(M_HEADS, LANE), seq3(1, kvrank),
                full(gk_p), full(wukt), full(wuv), hbm, hbm]
    wext_rows = wukt.shape[0] + 2 * M_HEADS
    return pl.pallas_call(
        functools.partial(_sample_attn_kernel, layer, n_pages, ppt, tpi),
        out_shape=jax.ShapeDtypeStruct((s_n, 1, vw), F32),
        grid_spec=pltpu.PrefetchScalarGridSpec(
            num_scalar_prefetch=1, grid=(s_n,), in_specs=in_specs,
            out_specs=pl.BlockSpec((1, 1, vw), lambda s, pt: (s, 0, 0)),
            scratch_shapes=[pltpu.VMEM((2, n_pages * page, kvrank), F32),
                            pltpu.VMEM((2, n_pages, rope, page), F32),
                            pltpu.SemaphoreType.DMA((2,)), pltpu.SemaphoreType.DMA((2,)),
                            pltpu.VMEM((wext_rows, kvrank), BF16),
                            pltpu.VMEM((n_tiles, M_HEADS, tile), F32),
                            pltpu.VMEM((n_tiles, M_HEADS, tile), F32),
                            pltpu.VMEM((tpi, wext_rows, tile), F32),
                            pltpu.VMEM((tpi, wext_rows, tile), F32)]),
        compiler_params=_cparams(("arbitrary",)),
        name="sample_attention",
    )(page_table.reshape(-1), q.reshape(s_n, M_HEADS, LANE), qa.reshape(s_n, M_HEADS, kvrank),
      k_new.reshape(s_n, M_HEADS, LANE), c_new.reshape(s_n, 1, kvrank), gk_p, wukt, wuv,
      cache_kv, cache_krt).reshape(s_n, vw)


def _mixer_out_kernel(n_exp, gn_ref, a_ref, x_ref, gt_ref, sh_ref, sc_ref, gmix_ref, wout_ref, gffn_ref,
                      wr_ref, br_ref, cnt_in_ref, x1_ref, h2_ref, route_ref, wt_ref, cnt_ref, run_ref):
    @pl.when(pl.program_id(0) == 0)
    def _():
        run_ref[...] = cnt_in_ref[...]

    gw = gn_ref.shape[-1]
    an = _rms(a_ref[...].astype(F32), gmix_ref[...])
    y = _dot(gn_ref[...], wout_ref[:gw, :]) + _dot(an.astype(BF16), wout_ref[gw:, :])
    x1 = x_ref[...] + gt_ref[...] * y
    x1_ref[...] = x1
    h2 = _rms(x1, gffn_ref[...]) * (1.0 + sc_ref[...]) + sh_ref[...]
    _store_row_tiles(h2_ref, h2)
    tm = h2.shape[0]

    logits = _dot_nt(wr_ref[...], h2, precision=lax.Precision.HIGHEST)
    scores = jax.nn.sigmoid(logits)
    sel = scores + br_ref[...]
    per = n_exp // N_GROUPS
    best = None
    for g in range(N_GROUPS):
        a, b, c, d = [sel[g * per + i:g * per + i + 1, :] for i in range(per)]
        hi1, lo1 = jnp.maximum(a, b), jnp.minimum(a, b)
        hi2, lo2 = jnp.maximum(c, d), jnp.minimum(c, d)
        gs = jnp.maximum(hi1, hi2) + jnp.maximum(jnp.minimum(hi1, hi2), jnp.maximum(lo1, lo2))
        if best is None:
            best, grp = gs, jnp.zeros(gs.shape, jnp.int32)
        else:
            better = gs > best
            grp = jnp.where(better, g, grp)
            best = jnp.where(better, gs, best)
    erow = lax.broadcasted_iota(jnp.int32, (n_exp, tm), 0)
    selm = jnp.where(erow // per == grp, sel, -jnp.inf)
    m1 = jnp.max(selm, axis=0, keepdims=True)
    i1 = jnp.min(jnp.where(selm == m1, erow, n_exp), axis=0, keepdims=True)
    oh1 = erow == i1
    selm2 = jnp.where(oh1, -jnp.inf, selm)
    m2 = jnp.max(selm2, axis=0, keepdims=True)
    i2 = jnp.min(jnp.where(selm2 == m2, erow, n_exp), axis=0, keepdims=True)
    oh2 = erow == i2
    s1 = jnp.sum(jnp.where(oh1, scores, 0.0), axis=0, keepdims=True)
    s2 = jnp.sum(jnp.where(oh2, scores, 0.0), axis=0, keepdims=True)
    tot = s1 + s2
    w1, w2 = s1 / tot, s2 / tot

    oh = jnp.where(oh1 | oh2, 1.0, 0.0)
    tri = (lax.broadcasted_iota(jnp.int32, (tm, tm), 0)
           <= lax.broadcasted_iota(jnp.int32, (tm, tm), 1)).astype(BF16)
    before = _dot(oh.astype(BF16), tri) - oh + run_ref[:, 0:1]
    r1 = jnp.sum(jnp.where(oh1, before, 0.0), axis=0, keepdims=True)
    r2 = jnp.sum(jnp.where(oh2, before, 0.0), axis=0, keepdims=True)
    run = run_ref[...] + jnp.sum(oh, axis=1, keepdims=True)
    run_ref[...] = run
    cnt_ref[...] = run
    zero = jnp.zeros((1, tm), F32)
    route_ref[...] = jnp.concatenate([i1.astype(F32), i2.astype(F32), r1, r2, w1, w2, zero, zero], axis=0)
    wt_ref[...] = jnp.concatenate([w1, w2, jnp.zeros((LANE - 2, tm), F32)], axis=0).T


def _mixer_out(gn, attn, x, mods, g_mix_a, w_out_b, g_norm_ffn, w_router_t, b_router, cnt_in, *, sample, t_len,
               tm):
    n, d = x.shape
    gw = gn.shape[1]
    n_exp = w_router_t.shape[0]
    full = lambda a: pl.BlockSpec(a.shape, lambda i: (0,) * a.ndim)
    row = lambda w: pl.BlockSpec((tm, w), lambda i: (i, 0))
    if sample:
        mod_spec = lambda k: pl.BlockSpec((tm, d), lambda i, k=k: (i, k))
    else:
        tpb = t_len // tm
        mod_spec = lambda k: pl.BlockSpec((None, 1, d), lambda i, k=k: (i // tpb, 0, k))
    return pl.pallas_call(
        functools.partial(_mixer_out_kernel, n_exp),
        out_shape=[jax.ShapeDtypeStruct((n, d), F32), jax.ShapeDtypeStruct((n * (d // LANE), LANE), F32),
                   jax.ShapeDtypeStruct((8, n), F32), jax.ShapeDtypeStruct((n, LANE), F32),
                   jax.ShapeDtypeStruct((n_exp, LANE), F32)],
        grid=(n // tm,),
        in_specs=[row(gw), row(attn.shape[1]), row(d), mod_spec(2), mod_spec(3), mod_spec(4), full(g_mix_a),
                  full(w_out_b), full(g_norm_ffn), full(w_router_t), full(b_router), full(cnt_in)],
        out_specs=[row(d), pl.BlockSpec((tm * (d // LANE), LANE), lambda i: (i, 0)),
                   pl.BlockSpec((8, tm), lambda i: (0, i)), row(LANE),
                   pl.BlockSpec((n_exp, LANE), lambda i: (0, 0))],
        scratch_shapes=[pltpu.VMEM((n_exp, LANE), F32)],
        compiler_params=_cparams(("arbitrary",)),
        name="mixer_out_sample" if sample else "mixer_out_prompt",
    )(gn, attn, x, mods, mods, mods, g_mix_a, w_out_b, g_norm_ffn, w_router_t, b_router, cnt_in)


def _dispatch_kernel(tm, k, h_hbm, dest_hbm, xs_in, xs_hbm, hbuf, dsm, sem_h, sem_d, sem_r):
    del xs_in
    i = pl.program_id(0)
    n = pl.num_programs(0)
    slot = lax.rem(i, 2)
    other = 1 - slot

    def tile_loads(t, sl):
        rows = pl.ds(pl.multiple_of(t * (tm * k), tm * k), tm * k)
        return (pltpu.make_async_copy(h_hbm.at[rows], hbuf.at[sl], sem_h.at[sl]),
                pltpu.make_async_copy(dest_hbm.at[t], dsm.at[sl], sem_d.at[sl]))

    def row_copy(sl, r, c):
        dst = pl.multiple_of(dsm[sl, c * tm + r] * k, k)
        return pltpu.make_async_copy(hbuf.at[sl, pl.ds(r * k, k)], xs_hbm.at[pl.ds(dst, k)], sem_r.at[sl])

    def wait_rows(sl):
        for r in range(tm):
            row_copy(sl, r, 0).wait()
            row_copy(sl, r, 1).wait()

    @pl.when(i == 0)
    def _():
        for cp in tile_loads(0, 0):
            cp.start()

    @pl.when(i > 0)
    def _():
        wait_rows(other)

    @pl.when(i + 1 < n)
    def _():
        for cp in tile_loads(i + 1, other):
            cp.start()

    for cp in tile_loads(i, slot):
        cp.wait()
    for r in range(tm):
        row_copy(slot, r, 0).start(priority=0)
        row_copy(slot, r, 1).start(priority=1)

    @pl.when(i == n - 1)
    def _():
        wait_rows(slot)


def _dispatch(h2t, dest, xs, *, tm, k):
    n = h2t.shape[0] // k
    hbm = pl.BlockSpec(memory_space=pl.ANY)
    return pl.pallas_call(
        functools.partial(_dispatch_kernel, tm, k),
        out_shape=jax.ShapeDtypeStruct(xs.shape, xs.dtype),
        grid=(n // tm,),
        in_specs=[hbm, hbm, hbm],
        out_specs=hbm,
        scratch_shapes=[pltpu.VMEM((2, tm * k, LANE), F32), pltpu.SMEM((2, 2 * tm), jnp.int32),
                        pltpu.SemaphoreType.DMA((2,)), pltpu.SemaphoreType.DMA((2,)),
                        pltpu.SemaphoreType.DMA((2,))],
        input_output_aliases={2: 0},
        compiler_params=_cparams(("arbitrary",)),
        name="moe_dispatch",
    )(h2t, dest, xs)


def _experts_kernel(tm, te_ref, nu_ref, x_ref, wg_ref, wu_ref, wd_ref, y_ref):
    j = pl.program_id(0)

    @pl.when(j < nu_ref[0])
    def _():
        x = _load_row_tiles(x_ref, tm).astype(BF16)
        a = _dot(x, wg_ref[...].astype(BF16))
        b = _dot(x, wu_ref[...].astype(BF16))
        act = (a * jax.nn.sigmoid(a)) * b
        _store_row_tiles(y_ref, _dot(act.astype(BF16), wd_ref[...].astype(BF16)))

    @pl.when(j >= nu_ref[0])
    def _():
        y_ref[...] = jnp.zeros(y_ref.shape, y_ref.dtype)


def _experts(tile_expert, n_used, xs, w_gate, w_up, w_down, *, layer, tm):
    d, f = w_gate.shape[-2:]
    k = d // LANE
    p = xs.shape[0] // k
    x_spec = pl.BlockSpec((tm * k, LANE), lambda j, te, nu: (jnp.minimum(j, nu[0] - 1), 0))
    w_spec = lambda a, b: pl.BlockSpec((None, None, a, b), lambda j, te, nu: (layer, te[j], 0, 0))
    return pl.pallas_call(
        functools.partial(_experts_kernel, tm),
        out_shape=jax.ShapeDtypeStruct(xs.shape, F32),
        grid_spec=pltpu.PrefetchScalarGridSpec(
            num_scalar_prefetch=2, grid=(p // tm,),
            in_specs=[x_spec, w_spec(d, f), w_spec(d, f), w_spec(f, d)],
            out_specs=pl.BlockSpec((tm * k, LANE), lambda j, te, nu: (j, 0))),
        compiler_params=_cparams(("arbitrary",)),
        name="moe_experts",
    )(tile_expert, n_used, xs, w_gate, w_up, w_down)


def _combine_kernel(tm, k, x1_ref, gt_ref, wt_ref, dest_hbm, ys_hbm, o_ref, dsm, ybuf, sem_d, sem_r):
    i = pl.program_id(0)
    n = pl.num_programs(0)
    slot = lax.rem(i, 2)
    other = 1 - slot

    def dest_load(t):
        ds = lax.rem(t, 3)
        return pltpu.make_async_copy(dest_hbm.at[t], dsm.at[ds], sem_d.at[ds])

    def row_copy(t, sl, r, c):
        src = pl.multiple_of(dsm[lax.rem(t, 3), c * tm + r] * k, k)
        return pltpu.make_async_copy(ys_hbm.at[pl.ds(src, k)], ybuf.at[sl, c, pl.ds(r * k, k)], sem_r.at[sl])

    def start_rows(t, sl):
        for r in range(tm):
            row_copy(t, sl, r, 0).start(priority=0)
            row_copy(t, sl, r, 1).start(priority=1)

    @pl.when(i == 0)
    def _():
        dest_load(0).start()

        @pl.when(n > 1)
        def _():
            dest_load(1).start()

        dest_load(0).wait()
        start_rows(0, 0)

    @pl.when(i + 2 < n)
    def _():
        dest_load(i + 2).start()

    @pl.when(i + 1 < n)
    def _():
        dest_load(i + 1).wait()
        start_rows(i + 1, other)

    for r in range(tm):
        row_copy(i, slot, r, 0).wait()
        row_copy(i, slot, r, 1).wait()
    wt = wt_ref[...]
    y = (wt[:, 0:1] * _load_row_tiles(ybuf.at[slot, 0], tm) + wt[:, 1:2] * _load_row_tiles(ybuf.at[slot, 1], tm))
    o_ref[...] = x1_ref[...] + gt_ref[...] * y


def _combine(x1, mods, wt, dest, ys, *, sample, t_len, tm):
    n, d = x1.shape
    k = d // LANE
    row = lambda w: pl.BlockSpec((tm, w), lambda i: (i, 0))
    if sample:
        mod_spec = pl.BlockSpec((tm, d), lambda i: (i, 5))
    else:
        tpb = t_len // tm
        mod_spec = pl.BlockSpec((None, 1, d), lambda i: (i // tpb, 0, 5))
    return pl.pallas_call(
        functools.partial(_combine_kernel, tm, k),
        out_shape=jax.ShapeDtypeStruct((n, d), F32),
        grid=(n // tm,),
        in_specs=[row(d), mod_spec, row(LANE), pl.BlockSpec(memory_space=pl.ANY),
                  pl.BlockSpec(memory_space=pl.ANY)],
        out_specs=row(d),
        scratch_shapes=[pltpu.SMEM((3, 2 * tm), jnp.int32), pltpu.VMEM((2, 2, tm * k, LANE), F32),
                        pltpu.SemaphoreType.DMA((3,)), pltpu.SemaphoreType.DMA((2,))],
        compiler_params=_cparams(("arbitrary",)),
        name="moe_combine_sample" if sample else "moe_combine_prompt",
    )(x1, mods, wt, dest, ys)


def _route_tables(counts, routes, tms, tm_e, n_tiles):
    n_exp = counts.shape[0]
    cnt = counts.astype(jnp.int32)
    padded = ((cnt + tm_e - 1) // tm_e) * tm_e
    ends = jnp.cumsum(padded)
    off = ends - padded
    n_used = (ends[-1] // tm_e).reshape(1)
    tile_start = jnp.arange(n_tiles, dtype=jnp.int32) * tm_e
    tile_expert = jnp.minimum(jnp.sum(tile_start[:, None] >= ends[None, :], axis=1), n_exp - 1).astype(jnp.int32)
    eids = jnp.arange(n_exp, dtype=jnp.int32)[:, None]
    dests = []
    for route, tm in zip(routes, tms):
        n = route.shape[1]
        e = route[0:2].astype(jnp.int32)
        base = jnp.sum(jnp.where(e[:, None, :] == eids[None], off[None, :, None], 0), axis=1)
        dest = base + route[2:4].astype(jnp.int32)
        dests.append(dest.reshape(2, n // tm, tm).transpose(1, 0, 2).reshape(n // tm, 2 * tm))
    return tile_expert, n_used, dests


def _rope_tables(pos, rope):
    half = rope // 2
    freqs = ROPE_THETA ** (-jnp.arange(half, dtype=F32) / half)
    ang = pos.astype(F32)[:, None] * freqs[None, :]
    cos, sin = jnp.cos(ang), jnp.sin(ang)
    t = pos.shape[0]
    pad = LANE - QK_NOPE - rope
    cq = jnp.concatenate([jnp.ones((t, QK_NOPE), F32), cos, cos, jnp.zeros((t, pad), F32)], axis=-1)
    sq = jnp.concatenate([jnp.zeros((t, QK_NOPE), F32), sin, sin, jnp.zeros((t, pad), F32)], axis=-1)
    return cq, sq


def _rot_cols(w, rope):
    half = rope // 2
    return jnp.concatenate([-w[..., half:], w[..., :half]], axis=-1)


def _prep_layer(l, w_in, w_uq, w_ukv, g_qk_q, g_qk_k, rope, gw, qrank, kvrank, vhead):
    d = w_in.shape[1]
    qk_head = QK_NOPE + rope
    pad = LANE - qk_head
    o = 2 * gw + qrank + kvrank
    w_kr = w_in[l][:, o:o + rope]
    z64 = jnp.zeros((d, QK_NOPE), F32)
    zp = jnp.zeros((d, pad), F32)
    w_in_p = jnp.concatenate([w_in[l][:, :o], z64, w_kr, zp, z64, _rot_cols(w_kr, rope), zp], axis=-1).astype(BF16)
    wq = w_uq[l].reshape(qrank, M_HEADS, qk_head)
    zq = jnp.zeros((qrank, M_HEADS, pad), F32)
    wq_plain = jnp.concatenate([wq, zq], axis=-1).reshape(qrank, M_HEADS * LANE)
    wq_rot = jnp.concatenate([jnp.zeros((qrank, M_HEADS, QK_NOPE), F32), _rot_cols(wq[..., QK_NOPE:], rope), zq],
                             axis=-1).reshape(qrank, M_HEADS * LANE)
    w_uq_p = jnp.concatenate([wq_plain, wq_rot], axis=-1).astype(BF16)
    wkv = w_ukv[l].reshape(kvrank, M_HEADS, QK_NOPE + vhead)
    wk = wkv[..., :QK_NOPE]
    wk_pad = jnp.concatenate([wk, jnp.zeros((kvrank, M_HEADS, LANE - QK_NOPE), F32)], axis=-1)
    wv = wkv[..., QK_NOPE:].reshape(kvrank, M_HEADS * vhead)
    w_kv_p = jnp.concatenate([wk_pad.reshape(kvrank, M_HEADS * LANE), wv], axis=-1).astype(BF16)
    wukt_pad = jnp.concatenate([wk.transpose(1, 2, 0), jnp.zeros((M_HEADS, LANE - QK_NOPE, kvrank), F32)],
                               axis=1).astype(BF16)
    wukt = wk.transpose(1, 2, 0).reshape(M_HEADS * QK_NOPE, kvrank).astype(BF16)
    zg = jnp.zeros((pad,), F32)
    gq_p = (jnp.concatenate([g_qk_q[l], zg]) * (qk_head ** -0.5)).reshape(1, LANE)
    gk_p = jnp.concatenate([g_qk_k[l], zg]).reshape(1, LANE)
    return w_in_p, w_uq_p, w_kv_p, wukt_pad, wukt, wv.astype(BF16), gq_p, gk_p


def kernel(x_prompt, x_sample, c_prompt, c_sample, cache_kv_latent, cache_k_rope, page_table, w_ada, b_ada,
           g_norm_mix, g_norm_ffn, w_in, g_v, w_s, b_s, g_cq, w_uq, g_ckv, w_ukv, g_qk_q, g_qk_k, g_mix_out,
           w_out, w_router, b_router, w_gate, w_up, w_down):
    batch, t_p, d = x_prompt.shape
    s_n, t_s, _ = x_sample.shape
    depth = w_ada.shape[0]
    gheads, chunk = w_s.shape[1], w_s.shape[2]
    hd = g_v.shape[-1]
    gw = gheads * hd
    qrank = g_cq.shape[-1]
    kvrank = g_ckv.shape[-1]
    rope = cache_k_rope.shape[-1]
    vhead = w_ukv.shape[-1] // M_HEADS - QK_NOPE
    past = page_table.shape[1] * cache_kv_latent.shape[2]
    assert t_s == 1 and t_p % chunk == 0 and g_qk_q.shape[-1] == QK_NOPE + rope
    row2 = lambda a: a.reshape(1, -1)

    mods = _ada(jnp.concatenate([c_prompt, c_sample], axis=0), w_ada, b_ada)
    tab_p = _rope_tables(jnp.arange(t_p, dtype=jnp.int32), rope)
    tab_s = _rope_tables(past + jnp.arange(t_s, dtype=jnp.int32), rope)
    w_router_t = w_router.T
    b_router_c = b_router.reshape(-1, 1)

    tm_p = 512 if t_p % 512 == 0 else 256
    tm_d = 256
    tm_s = s_n
    n_exp = w_router.shape[1]
    tm_e = 512
    n_pairs = 2 * (batch * t_p + s_n)
    n_tiles_e = -(-n_pairs // tm_e) + n_exp
    tq = 512 if t_p % 512 == 0 else 256
    cache_krt = jnp.swapaxes(cache_k_rope, 2, 3)
    xp = x_prompt.reshape(batch * t_p, d)
    xs = x_sample.reshape(s_n, d)
    open_p = ((t_p - 1) // chunk) * chunk
    outs = [[] for _ in range(6)]
    for l in range(depth):
        w_in_p, w_uq_p, w_kv_p, wukt_pad, wukt, wuv, gq_p, gk_p = _prep_layer(
            l, w_in, w_uq, w_ukv, g_qk_q, g_qk_k, rope, gw, qrank, kvrank, vhead)
        g_mix_g, g_mix_a = row2(g_mix_out[l, :gw]), row2(g_mix_out[l, gw:])
        w_out_b = w_out[l].astype(BF16)
        base = (row2(g_norm_mix[l]), w_in_p, row2(g_v[l]), row2(g_cq[l]), w_uq_p, row2(g_ckv[l]), w_kv_p,
                gq_p, gk_p, g_mix_g)
        mp = mods[l, :batch].reshape(batch, 1, N_MOD * d)
        ms = mods[l, batch:]

        w00 = row2(jnp.repeat(w_s[l, :, 0, 0], hd))
        b0 = row2(jnp.repeat(b_s[l, :, 0], hd))
        gn, v, q, ckv, kr, k, qa = _mixer_in(xs, ms, base + ((w00, b0, wukt_pad),), tab_s,
                                             sample=True, t_len=t_s, tm=tm_s, rope=rope)
        attn = _sample_attn(page_table, q, qa, k, ckv, gk_p, wukt, wuv, cache_kv_latent, cache_krt, layer=l)
        x1s, h2s, route_s, wt_s, cnt = _mixer_out(gn, attn, xs, ms, g_mix_a, w_out_b, row2(g_norm_ffn[l]),
                                                  w_router_t, b_router_c, jnp.zeros((n_exp, LANE), F32),
                                                  sample=True, t_len=t_s, tm=tm_s)
        outs[3].append(ckv.reshape(s_n, t_s, kvrank))
        outs[4].append(kr.reshape(s_n, t_s, rope))
        outs[5].append(v.reshape(s_n, t_s, gheads, hd))

        bs_full = jnp.broadcast_to(b_s[l][:, :, None], (gheads, chunk, hd))
        gn, v, q, ckv, kr, k, vv = _mixer_in(xp, mp, base + ((w_s[l], bs_full),), tab_p,
                                             sample=False, t_len=t_p, tm=tm_p, rope=rope)
        attn = _prompt_attn(q, k, vv, batch=batch, t_len=t_p, tq=tq)
        x1p, h2p, route_p, wt_p, cnt = _mixer_out(gn, attn, xp, mp, g_mix_a, w_out_b, row2(g_norm_ffn[l]),
                                                  w_router_t, b_router_c, cnt, sample=False, t_len=t_p, tm=tm_p)
        outs[0].append(ckv.reshape(batch, t_p, kvrank))
        outs[1].append(kr.reshape(batch, t_p, rope))
        outs[2].append(v.reshape(batch, t_p, gw)[:, open_p:].reshape(batch, t_p - open_p, gheads, hd))

        tile_expert, n_used, (dest_s, dest_p) = _route_tables(
            cnt[:, 0], (route_s, route_p), (tm_s, tm_d), tm_e, n_tiles_e)
        if l == 0:
            xsort = jnp.zeros((n_tiles_e * tm_e * (d // LANE), LANE), F32)
        xsort = _dispatch(h2s, dest_s, xsort, tm=tm_s, k=d // LANE)
        xsort = _dispatch(h2p, dest_p, xsort, tm=tm_d, k=d // LANE)
        ysort = _experts(tile_expert, n_used, xsort, w_gate, w_up, w_down, layer=l, tm=tm_e)
        xs = _combine(x1s, ms, wt_s, dest_s, ysort, sample=True, t_len=t_s, tm=tm_s)
        xp = _combine(x1p, mp, wt_p, dest_p, ysort, sample=False, t_len=t_p, tm=tm_d)

    return (xp.reshape(batch, t_p, d), xs.reshape(s_n, t_s, d), jnp.stack(outs[0]), jnp.stack(outs[1]),
            jnp.stack(outs[2]), jnp.stack(outs[3]), jnp.stack(outs[4]), jnp.stack(outs[5]))
```
